```python
import jax, jax.numpy as jnp
from jax import lax
import numpy as np

D_MODEL = 1024
BATCH = 32
SEQ = 2048
DEPTH = 1
DEC_BATCH = 128
DEC_SEQ = 4
PAST_LEN = 8192
PAGE_SIZE = 128

ATT_HEADS = 8
ATT_HEAD_DIM = 64
ATT_WIDTH = ATT_HEADS * ATT_HEAD_DIM
MOBA_BLOCK = 256
MOBA_TOPK = 3
Q_BLOCK = 16
LSTM_HEADS = 4
LSTM_HEAD_DIM = 128
LSTM_WIDTH = LSTM_HEADS * LSTM_HEAD_DIM
CONV_WIDTH = 4
LSTM_CHUNK = 128
MIX_WIDTH = ATT_WIDTH + LSTM_WIDTH
SPLITS = (ATT_WIDTH, 2 * ATT_WIDTH, 3 * ATT_WIDTH, 3 * ATT_WIDTH + 2 * LSTM_WIDTH,
          3 * ATT_WIDTH + 3 * LSTM_WIDTH, 3 * ATT_WIDTH + 4 * LSTM_WIDTH,
          3 * ATT_WIDTH + 4 * LSTM_WIDTH + LSTM_HEADS)
IN_COLS = 3 * ATT_WIDTH + 4 * LSTM_WIDTH + 2 * LSTM_HEADS
N_EXPERTS = 32
TOP_K = 4
D_FF = D_MODEL
SWIGLU_LIMIT = 7.0
SWIGLU_ALPHA = 1.702
MOE_BLOCK = 256
NORM_EPS = 1e-5

kernel_name = "hymba_moba_mlstm_gptoss_moe_step"


def rmsnorm(x, g):
    xf = x.astype(jnp.float32)
    y = xf * lax.rsqrt(jnp.mean(xf * xf, axis=-1, keepdims=True) + NORM_EPS)
    return (y * g.astype(jnp.float32)).astype(x.dtype)


def alibi_slopes():
    return 2.0 ** (-8.0 * jnp.arange(1, ATT_HEADS + 1, dtype=jnp.float32) / ATT_HEADS)


def causal_conv(u, buf, w, b):
    L = u.shape[1]
    full = jnp.concatenate([buf.astype(u.dtype), u], axis=1)
    y = b + full[:, 0:L] * w[0]
    for j in range(1, CONV_WIDTH):
        y = y + full[:, j:j + L] * w[j]
    return jax.nn.silu(y), full[:, L:]


def mixer_front(x, lp, conv_buf):
    B, L = x.shape[:2]
    h = rmsnorm(x, lp["norm_mix"])
    z = h @ lp["w_in"]
    q_a, k_a, v_a, qk_src, v_l, o_l, i_l, f_l = jnp.split(z, SPLITS, axis=-1)
    att = lambda t: t.reshape(B, L, ATT_HEADS, ATT_HEAD_DIM)
    qk, new_buf = causal_conv(qk_src, conv_buf, lp["conv_w"], lp["conv_b"])
    lst = lambda t: t.astype(jnp.float32).reshape(B, L, LSTM_HEADS, LSTM_HEAD_DIM).transpose(0, 2, 1, 3)
    q_l = lst(qk[..., :LSTM_WIDTH])
    k_l = lst(qk[..., LSTM_WIDTH:]) * (LSTM_HEAD_DIM ** -0.5)
    g = (jnp.concatenate([i_l, f_l], axis=-1).astype(jnp.float32) + lp["gate_b"].astype(jnp.float32)).transpose(0, 2, 1)
    i_pre = g[:, :LSTM_HEADS]
    log_f = jax.nn.log_sigmoid(g[:, LSTM_HEADS:])
    return (att(q_a), att(k_a), att(v_a)), (q_l, k_l, lst(v_l), i_pre, log_f), o_l, new_buf


def mlstm_chunk(carry, inp):
    C0, n0, m0 = carry
    q, k, v, ig, lf = inp
    L = q.shape[2]
    b = jnp.cumsum(lf, axis=-1)
    causal = jnp.tril(jnp.ones((L, L), dtype=bool))
    logD = jnp.where(causal, b[..., :, None] - b[..., None, :] + ig[..., None, :], -jnp.inf)
    dec0 = b + m0[..., None]
    m = jnp.maximum(dec0, jnp.max(logD, axis=-1))
    Dm = jnp.exp(logD - m[..., None])
    w0 = jnp.exp(dec0 - m)
    S = jnp.einsum("bhtd,bhsd->bhts", q, k) * Dm
    num = w0[..., None] * jnp.einsum("bhvd,bhtd->bhtv", C0, q) + jnp.einsum("bhts,bhsv->bhtv", S, v)
    den = w0 * jnp.einsum("bhd,bhtd->bht", n0, q) + jnp.sum(S, axis=-1)
    h = num / jnp.maximum(jnp.abs(den), jnp.exp(-m))[..., None]
    mL = m[..., -1]
    wL = jnp.exp(b[..., -1:] - b + ig - mL[..., None])
    w0L = jnp.exp(b[..., -1] + m0 - mL)
    C = w0L[..., None, None] * C0 + jnp.einsum("bhs,bhsv,bhsd->bhvd", wL, v, k)
    n = w0L[..., None] * n0 + jnp.einsum("bhs,bhsd->bhd", wL, k)
    return (C, n, mL), h


def mlstm_prompt(q, k, v, ig, lf):
    B, NH, S, d = q.shape
    nc = S // LSTM_CHUNK
    ch = lambda t: jnp.moveaxis(t.reshape(t.shape[:2] + (nc, LSTM_CHUNK) + t.shape[3:]), 2, 0)
    init = (jnp.zeros((B, NH, d, d), jnp.float32), jnp.zeros((B, NH, d), jnp.float32),
            jnp.zeros((B, NH), jnp.float32))
    state, hs = lax.scan(mlstm_chunk, init, (ch(q), ch(k), ch(v), ch(ig), ch(lf)))
    return state, jnp.moveaxis(hs, 0, 2).reshape(B, NH, S, d)


def gather_blocks(blocks, idx):
    return jax.vmap(jax.vmap(lambda bl, ix: bl[ix]))(blocks, idx)


def moba_prompt(q, k, v, slopes):
    B, S = q.shape[:2]
    nb = -(-S // MOBA_BLOCK)
    sp = nb * MOBA_BLOCK
    pad = ((0, 0), (0, sp - S), (0, 0), (0, 0))
    qh = jnp.pad(q, pad).transpose(0, 2, 1, 3)
    kb = jnp.pad(k, pad).transpose(0, 2, 1, 3).reshape(B, ATT_HEADS, nb, MOBA_BLOCK, ATT_HEAD_DIM)
    vb = jnp.pad(v, pad).transpose(0, 2, 1, 3).reshape(B, ATT_HEADS, nb, MOBA_BLOCK, ATT_HEAD_DIM)
    k_sel = min(MOBA_TOPK, nb - 1)
    q_blk = jnp.arange(sp) // MOBA_BLOCK
    scale = ATT_HEAD_DIM ** -0.5
    if k_sel > 0:
        k_mean = jnp.mean(kb.astype(jnp.float32), axis=3)
        gate = jnp.einsum("bhtd,bhjd->bhtj", qh.astype(jnp.float32), k_mean)
        gate = jnp.where(jnp.arange(nb)[None, :] < q_blk[:, None], gate, -jnp.inf)
        _, sel = lax.top_k(gate, k_sel)
        valid = jnp.arange(k_sel)[None, :] < q_blk[:, None]

    def step(c):
        t0 = c * Q_BLOCK
        tq = t0 + jnp.arange(Q_BLOCK)
        qc = lax.dynamic_slice_in_dim(qh, t0, Q_BLOCK, axis=2)
        ob = t0 // MOBA_BLOCK
        ko = lax.dynamic_index_in_dim(kb, ob, axis=2, keepdims=False)
        vo = lax.dynamic_index_in_dim(vb, ob, axis=2, keepdims=False)
        pos_o = ob * MOBA_BLOCK + jnp.arange(MOBA_BLOCK)
        s_own = (jnp.einsum("bhqd,bhsd->bhqs", qc, ko).astype(jnp.float32) * scale
                 - slopes[:, None, None] * (tq[:, None] - pos_o[None, :]).astype(jnp.float32))
        s_own = jnp.where(pos_o[None, :] <= tq[:, None], s_own, -jnp.inf)
        if k_sel == 0:
            p = jax.nn.softmax(s_own, axis=-1).astype(vo.dtype)
            return jnp.einsum("bhqs,bhsd->bhqd", p, vo)
        selc = lax.dynamic_slice_in_dim(sel, t0, Q_BLOCK, axis=2)
        validc = lax.dynamic_slice_in_dim(valid, t0, Q_BLOCK, axis=0)
        kg = gather_blocks(kb, selc)
        vg = gather_blocks(vb, selc)
        pos_s = selc[..., None] * MOBA_BLOCK + jnp.arange(MOBA_BLOCK)
        s_sel = (jnp.einsum("bhqd,bhqksd->bhqks", qc, kg).astype(jnp.float32) * scale
                 - slopes[:, None, None, None] * (tq[:, None, None] - pos_s).astype(jnp.float32))
        s_sel = jnp.where(validc[:, :, None], s_sel, -jnp.inf)
        n_sel = k_sel * MOBA_BLOCK
        s = jnp.concatenate([s_sel.reshape(B, ATT_HEADS, Q_BLOCK, n_sel), s_own], axis=-1)
        p = jax.nn.softmax(s, axis=-1).astype(vo.dtype)
        p_sel = p[..., :n_sel].reshape(B, ATT_HEADS, Q_BLOCK, k_sel, MOBA_BLOCK)
        return (jnp.einsum("bhqks,bhqksd->bhqd", p_sel, vg)
                + jnp.einsum("bhqs,bhsd->bhqd", p[..., n_sel:], vo))

    out = lax.map(step, jnp.arange(sp // Q_BLOCK))
    out = jnp.moveaxis(out, 0, 2).reshape(B, ATT_HEADS, sp, ATT_HEAD_DIM)[:, :, :S]
    return out.transpose(0, 2, 1, 3)


def moba_sample(q, k, v, cache_k, cache_v, page_table, slopes):
    DB, DS = q.shape[:2]
    past = page_table.shape[1] * PAGE_SIZE
    ppb = MOBA_BLOCK // PAGE_SIZE
    n_full = past // MOBA_BLOCK
    own_start = n_full * MOBA_BLOCK
    k_sel = min(MOBA_TOPK, n_full)
    scale = ATT_HEAD_DIM ** -0.5
    qh = q.transpose(0, 2, 1, 3)
    tq = past + jnp.arange(DS)
    own_pages = page_table[:, n_full * ppb:]
    n_own = own_pages.shape[1] * PAGE_SIZE
    rows = lambda pool: pool[own_pages].transpose(0, 2, 1, 3, 4).reshape(DB, ATT_HEADS, n_own, ATT_HEAD_DIM)
    k_loc = jnp.concatenate([rows(cache_k).astype(k.dtype), k.transpose(0, 2, 1, 3)], axis=2)
    v_loc = jnp.concatenate([rows(cache_v).astype(v.dtype), v.transpose(0, 2, 1, 3)], axis=2)
    pos_loc = jnp.concatenate([jnp.arange(own_start, past), tq])
    s_loc = (jnp.einsum("bhqd,bhsd->bhqs", qh, k_loc).astype(jnp.float32) * scale
             - slopes[:, None, None] * (tq[:, None] - pos_loc[None, :]).astype(jnp.float32))
    s_loc = jnp.where(pos_loc[None, :] <= tq[:, None], s_loc, -jnp.inf)
    if k_sel == 0:
        p = jax.nn.softmax(s_loc, axis=-1).astype(v_loc.dtype)
        return jnp.einsum("bhqs,bhsd->bhqd", p, v_loc).transpose(0, 2, 1, 3)
    blk_pages = page_table[:, :n_full * ppb]
    k_mean = cache_k[blk_pages].astype(jnp.float32).reshape(
        DB, n_full, ppb, ATT_HEADS, PAGE_SIZE, ATT_HEAD_DIM).mean(axis=(2, 4))
    gate = jnp.einsum("bhqd,bjhd->bhqj", qh.astype(jnp.float32), k_mean)
    _, sel = lax.top_k(gate, k_sel)
    pg = page_table[jnp.arange(DB)[:, None, None, None, None], sel[..., None] * ppb + jnp.arange(ppb)]
    h_idx = jnp.arange(ATT_HEADS)[None, :, None, None, None]
    kg = cache_k[pg, h_idx].reshape(DB, ATT_HEADS, DS, k_sel, MOBA_BLOCK, ATT_HEAD_DIM).astype(k.dtype)
    vg = cache_v[pg, h_idx].reshape(DB, ATT_HEADS, DS, k_sel, MOBA_BLOCK, ATT_HEAD_DIM).astype(v.dtype)
    pos_s = sel[..., None] * MOBA_BLOCK + jnp.arange(MOBA_BLOCK)
    s_sel = (jnp.einsum("bhqd,bhqksd->bhqks", qh, kg).astype(jnp.float32) * scale
             - slopes[:, None, None, None] * (tq[:, None, None] - pos_s).astype(jnp.float32))
    n_sel = k_sel * MOBA_BLOCK
    s = jnp.concatenate([s_sel.reshape(DB, ATT_HEADS, DS, n_sel), s_loc], axis=-1)
    p = jax.nn.softmax(s, axis=-1).astype(v.dtype)
    p_sel = p[..., :n_sel].reshape(DB, ATT_HEADS, DS, k_sel, MOBA_BLOCK)
    out = (jnp.einsum("bhqks,bhqksd->bhqd", p_sel, vg)
           + jnp.einsum("bhqs,bhsd->bhqd", p[..., n_sel:], v_loc))
    return out.transpose(0, 2, 1, 3)


def moe_ffn(x, lp):
    T = x.shape[0]
    logits = (x @ lp["w_router"]).astype(jnp.float32) + lp["b_router"].astype(jnp.float32)
    top_val, top_idx = lax.top_k(logits, TOP_K)
    gates = jax.nn.softmax(top_val, axis=-1)
    e_flat = top_idx.reshape(-1)
    tok_flat = jnp.repeat(jnp.arange(T), TOP_K)
    order = jnp.argsort(e_flat)
    e_sorted = e_flat[order]
    tok_sorted = tok_flat[order]
    g_sorted = gates.reshape(-1)[order]
    counts = jnp.bincount(e_flat, length=N_EXPERTS)
    padded = ((counts + MOE_BLOCK - 1) // MOE_BLOCK) * MOE_BLOCK
    pad_end = jnp.cumsum(padded)
    pad_start = pad_end - padded
    grp_start = jnp.cumsum(counts) - counts
    dest = pad_start[e_sorted] + (jnp.arange(T * TOP_K) - grp_start[e_sorted])
    n_blocks = -(-(T * TOP_K) // MOE_BLOCK) + N_EXPERTS
    cap = n_blocks * MOE_BLOCK
    xs = jnp.zeros((cap, D_MODEL), x.dtype).at[dest].set(x[tok_sorted])
    blk_expert = jnp.minimum(jnp.searchsorted(pad_end, jnp.arange(n_blocks) * MOE_BLOCK, side="right"),
                             N_EXPERTS - 1)
    w_up, b_up, w_down, b_down = lp["w_up"], lp["b_up"], lp["w_down"], lp["b_down"]

    def expert_block(args):
        xb, e = args
        gu = (xb @ w_up[e] + b_up[e]).astype(jnp.float32)
        x_glu = jnp.minimum(gu[:, :D_FF], SWIGLU_LIMIT)
        x_lin = jnp.clip(gu[:, D_FF:], -SWIGLU_LIMIT, SWIGLU_LIMIT)
        hmid = (x_glu * jax.nn.sigmoid(SWIGLU_ALPHA * x_glu) * (x_lin + 1.0)).astype(xb.dtype)
        return hmid @ w_down[e] + b_down[e]

    ys = lax.map(expert_block, (xs.reshape(n_blocks, MOE_BLOCK, D_MODEL), blk_expert)).reshape(cap, D_MODEL)
    contrib = (ys[dest].astype(jnp.float32) * g_sorted[:, None]).astype(x.dtype)
    return jax.ops.segment_sum(contrib, tok_sorted, num_segments=T)


def mixer_back(x, att_out, h_l, o_l, lp):
    B, L = x.shape[:2]
    hn = h_l * lax.rsqrt(jnp.mean(h_l * h_l, axis=-1, keepdims=True) + NORM_EPS)
    hn = (hn.transpose(0, 2, 1, 3).reshape(B, L, LSTM_WIDTH) * lp["lstm_norm"].astype(jnp.float32)
          * jax.nn.sigmoid(o_l.astype(jnp.float32)))
    mix = jnp.concatenate([att_out.reshape(B, L, ATT_WIDTH).astype(x.dtype), hn.astype(x.dtype)], axis=-1)
    x = x + mix @ lp["w_out"]
    y = moe_ffn(rmsnorm(x, lp["norm_ffn"]).reshape(B * L, D_MODEL), lp)
    return x + y.reshape(B, L, D_MODEL)


def to_pages(t):
    B, S = t.shape[:2]
    return t.reshape(B, S // PAGE_SIZE, PAGE_SIZE, ATT_HEADS, ATT_HEAD_DIM).transpose(0, 1, 3, 2, 4)


def prompt_layer(x, lp, slopes):
    conv0 = jnp.zeros((x.shape[0], CONV_WIDTH - 1, 2 * LSTM_WIDTH), x.dtype)
    (q_a, k_a, v_a), lst, o_l, conv_new = mixer_front(x, lp, conv0)
    att = moba_prompt(q_a, k_a, v_a, slopes)
    (C, n, m), h_l = mlstm_prompt(*lst)
    y = mixer_back(x, att, h_l, o_l, lp)
    return y, (to_pages(k_a), to_pages(v_a), conv_new, C, n, m)


def sample_layer(x, lp, slopes, cache_k, cache_v, page_table, conv_buf, C, n, m):
    (q_a, k_a, v_a), lst, o_l, conv_new = mixer_front(x, lp, conv_buf)
    att = moba_sample(q_a, k_a, v_a, cache_k, cache_v, page_table, slopes)
    carry = (C.astype(jnp.float32), n.astype(jnp.float32), m.astype(jnp.float32))
    (C1, n1, m1), h_l = mlstm_chunk(carry, lst)
    y = mixer_back(x, att, h_l, o_l, lp)
    return y, (k_a.transpose(0, 2, 1, 3), v_a.transpose(0, 2, 1, 3), conv_new, C1, n1, m1)


def setup_inputs(seed: int = 0) -> dict:
    key = jax.random.key(seed)
    ks = jax.random.split(key, 24)
    f32 = jnp.float32
    nrm = lambda k, shape, s: s * jax.random.normal(k, shape, f32)
    n_pages = PAST_LEN // PAGE_SIZE
    n_pool = (DEC_BATCH * n_pages * 5) // 4
    page_table = jax.random.permutation(ks[4], n_pool)[:DEC_BATCH * n_pages].reshape(
        DEC_BATCH, n_pages).astype(jnp.int32)
    gate_b = jnp.concatenate([nrm(ks[9], (DEPTH, LSTM_HEADS), 0.1),
                              3.0 + nrm(ks[10], (DEPTH, LSTM_HEADS), 0.5)], axis=-1)
    return {
        "x_prompt": nrm(ks[0], (BATCH, SEQ, D_MODEL), 1.0),
        "x_sample": nrm(ks[1], (DEC_BATCH, DEC_SEQ, D_MODEL), 1.0),
        "cache_k": nrm(ks[2], (DEPTH, n_pool, ATT_HEADS, PAGE_SIZE, ATT_HEAD_DIM), 1.0),
        "cache_v": nrm(ks[3], (DEPTH, n_pool, ATT_HEADS, PAGE_SIZE, ATT_HEAD_DIM), 1.0),
        "page_table": page_table,
        "state_conv": nrm(ks[5], (DEPTH, DEC_BATCH, CONV_WIDTH - 1, 2 * LSTM_WIDTH), 1.0),
        "state_C": nrm(ks[6], (DEPTH, DEC_BATCH, LSTM_HEADS, LSTM_HEAD_DIM, LSTM_HEAD_DIM), 0.1),
        "state_n": nrm(ks[7], (DEPTH, DEC_BATCH, LSTM_HEADS, LSTM_HEAD_DIM), 0.5),
        "state_m": nrm(ks[8], (DEPTH, DEC_BATCH, LSTM_HEADS), 1.0),
        "norm_mix": 1.0 + nrm(ks[11], (DEPTH, D_MODEL), 0.02),
        "w_in": nrm(ks[12], (DEPTH, D_MODEL, IN_COLS), D_MODEL ** -0.5),
        "conv_w": nrm(ks[13], (DEPTH, CONV_WIDTH, 2 * LSTM_WIDTH), CONV_WIDTH ** -0.5),
        "conv_b": nrm(ks[14], (DEPTH, 2 * LSTM_WIDTH), 0.02),
        "gate_b": gate_b,
        "lstm_norm": 1.0 + nrm(ks[15], (DEPTH, LSTM_WIDTH), 0.02),
        "w_out": nrm(ks[16], (DEPTH, MIX_WIDTH, D_MODEL), MIX_WIDTH ** -0.5),
        "norm_ffn": 1.0 + nrm(ks[17], (DEPTH, D_MODEL), 0.02),
        "w_router": nrm(ks[18], (DEPTH, D_MODEL, N_EXPERTS), D_MODEL ** -0.5),
        "b_router": nrm(ks[19], (DEPTH, N_EXPERTS), 0.01),
        "w_up": nrm(ks[20], (DEPTH, N_EXPERTS, D_MODEL, 2 * D_FF), D_MODEL ** -0.5),
        "b_up": nrm(ks[21], (DEPTH, N_EXPERTS, 2 * D_FF), 0.02),
        "w_down": nrm(ks[22], (DEPTH, N_EXPERTS, D_FF, D_MODEL), D_FF ** -0.5),
        "b_down": nrm(ks[23], (DEPTH, N_EXPERTS, D_MODEL), 0.02),
        "norm_final": 1.0 + nrm(jax.random.fold_in(key, 99), (D_MODEL,), 0.02),
    }


def reference(x_prompt, x_sample, cache_k, cache_v, page_table, state_conv, state_C, state_n, state_m,
              norm_mix, w_in, conv_w, conv_b, gate_b, lstm_norm, w_out, norm_ffn, w_router, b_router,
              w_up, b_up, w_down, b_down, norm_final):
    slopes = alibi_slopes()
    xp, xs = x_prompt, x_sample
    p_states, s_states = [], []
    for l in range(DEPTH):
        lp = {"norm_mix": norm_mix[l], "w_in": w_in[l], "conv_w": conv_w[l], "conv_b": conv_b[l],
              "gate_b": gate_b[l], "lstm_norm": lstm_norm[l], "w_out": w_out[l], "norm_ffn": norm_ffn[l],
              "w_router": w_router[l], "b_router": b_router[l], "w_up": w_up[l], "b_up": b_up[l],
              "w_down": w_down[l], "b_down": b_down[l]}
        xp, sp_l = prompt_layer(xp, lp, slopes)
        xs, ss_l = sample_layer(xs, lp, slopes, cache_k[l], cache_v[l], page_table,
                                state_conv[l], state_C[l], state_n[l], state_m[l])
        p_states.append(sp_l)
        s_states.append(ss_l)
    k_prompt, v_prompt, conv_prompt, C_prompt, n_prompt, m_prompt = [jnp.stack(s) for s in zip(*p_states)]
    k_sample, v_sample, conv_sample, C_sample, n_sample, m_sample = [jnp.stack(s) for s in zip(*s_states)]
    y_prompt = rmsnorm(xp, norm_final)
    y_sample = rmsnorm(xs, norm_final)
    return (y_prompt, y_sample, k_prompt, v_prompt, k_sample, v_sample, conv_prompt, conv_sample,
            C_prompt, n_prompt, m_prompt, C_sample, n_sample, m_sample)
```

```python
import functools

import jax
import jax.numpy as jnp
from jax import lax
from jax.experimental import pallas as pl
from jax.experimental.pallas import tpu as pltpu

F32 = jnp.float32
BF16 = jnp.bfloat16

D_MODEL = 1024
ATT_HEADS = 8
ATT_HEAD_DIM = 64
ATT_WIDTH = ATT_HEADS * ATT_HEAD_DIM
MOBA_BLOCK = 256
MOBA_TOPK = 3
LSTM_HEADS = 4
LSTM_HEAD_DIM = 128
LSTM_WIDTH = LSTM_HEADS * LSTM_HEAD_DIM
CONV_WIDTH = 4
LSTM_CHUNK = 128
PAGE_SIZE = 128
N_EXPERTS = 32
TOP_K = 4
D_FF = D_MODEL
SWIGLU_LIMIT = 7.0
SWIGLU_ALPHA = 1.702
NORM_EPS = 1e-5
MAIN_COLS = 3 * ATT_WIDTH + 4 * LSTM_WIDTH
COL_CHUNK = 512
LANES = 128
SUBLANES = 8
EXPERT_ROWS = 512
NEG = -1e30
VMEM_LIMIT = 48 * 1024 * 1024

_NT = (((1,), (1,)), ((), ()))
_HI = lax.Precision.HIGHEST


def _dot(a, b):
    return jnp.dot(a, b, preferred_element_type=F32)


def _dot_nt(a, b, precision=None):
    return lax.dot_general(a, b, _NT, precision=precision, preferred_element_type=F32)


def _params(sem):
    return pltpu.CompilerParams(dimension_semantics=sem, vmem_limit_bytes=VMEM_LIMIT)


def _front_body(n_p, xp_ref, xs_ref, g_ref, w_ref, wg_ref,
                qa_ref, ka_ref, va_ref, qk_ref, vl_ref, ol_ref, gt_ref):
    i = pl.program_id(0)
    x = jnp.where(i < n_p, xp_ref[...], xs_ref[...])
    ms = jnp.mean(x * x, axis=-1, keepdims=True)
    h = (x * lax.rsqrt(ms + NORM_EPS) * g_ref[...]).astype(BF16)
    dests = ((qa_ref, 0), (ka_ref, 0), (va_ref, 0), (qk_ref, 0), (qk_ref, 1), (vl_ref, 0), (ol_ref, 0))
    for c, (ref, part) in enumerate(dests):
        z = _dot(h, w_ref[:, c * COL_CHUNK:(c + 1) * COL_CHUNK])
        ref[:, part * COL_CHUNK:(part + 1) * COL_CHUNK] = z.astype(ref.dtype)
    gt_ref[...] = _dot(h, wg_ref[...])


def _front(xp, xs, g, w_main, w_gate):
    t_p, t_s = xp.shape[0], xs.shape[0]
    tm = t_s
    assert t_p % tm == 0 and tm % 128 == 0
    n_p = t_p // tm
    t_all = t_p + t_s
    row = lambda i: (i, 0)
    const = lambda i: (0, 0)
    wide = lambda n, dt: jax.ShapeDtypeStruct((t_all, n), dt)
    return pl.pallas_call(
        functools.partial(_front_body, n_p),
        grid=(n_p + 1,),
        in_specs=[pl.BlockSpec((tm, D_MODEL), lambda i: (jnp.minimum(i, n_p - 1), 0)),
                  pl.BlockSpec((tm, D_MODEL), const),
                  pl.BlockSpec((1, D_MODEL), const),
                  pl.BlockSpec((D_MODEL, MAIN_COLS), const),
                  pl.BlockSpec((D_MODEL, LANES), const)],
        out_specs=[pl.BlockSpec((tm, ATT_WIDTH), row), pl.BlockSpec((tm, ATT_WIDTH), row),
                   pl.BlockSpec((tm, ATT_WIDTH), row), pl.BlockSpec((tm, 2 * LSTM_WIDTH), row),
                   pl.BlockSpec((tm, LSTM_WIDTH), row), pl.BlockSpec((tm, LSTM_WIDTH), row),
                   pl.BlockSpec((tm, LANES), row)],
        out_shape=[wide(ATT_WIDTH, BF16), wide(ATT_WIDTH, F32), wide(ATT_WIDTH, F32),
                   wide(2 * LSTM_WIDTH, F32), wide(LSTM_WIDTH, F32), wide(LSTM_WIDTH, F32),
                   wide(LANES, F32)],
        compiler_params=_params(("arbitrary",)),
        name="front",
    )(xp, xs, g, w_main, w_gate)


def _topk_mask(g, n_valid, k_sel):
    n = g.shape[0]
    jj = lax.broadcasted_iota(jnp.int32, g.shape, 0)
    rank = jnp.zeros(g.shape, F32)
    for j2 in range(n):
        row = g[j2:j2 + 1, :]
        beats = jnp.where(row > g, 1.0, jnp.where((row == g) & (jj > j2), 1.0, 0.0))
        rank = rank + jnp.where(j2 < n_valid, beats, 0.0)
    return jnp.where((jj < n_valid) & (rank < k_sel), 1.0, 0.0)


def _moba_body(nb, q_ref, k_ref, v_ref, o_ref, kb, vb, kmean):
    i = pl.program_id(1)
    blk = MOBA_BLOCK
    scale = ATT_HEAD_DIM ** -0.5
    k_sel = min(MOBA_TOPK, nb - 1)

    @pl.when(i == 0)
    def _():
        for c in range(ATT_WIDTH // LANES):
            sl = slice(c * LANES, (c + 1) * LANES)
            kf = k_ref[:, sl]
            kb[:, sl] = kf.astype(BF16)
            vb[:, sl] = v_ref[:, sl].astype(BF16)
            kmean[c] = jnp.sum(kf.reshape(nb, blk, LANES), axis=1) * (1.0 / blk)

    r_io = lax.broadcasted_iota(jnp.int32, (blk, blk), 0)
    c_io = lax.broadcasted_iota(jnp.int32, (blk, blk), 1)
    rel = (c_io - r_io).astype(F32)
    causal = c_io <= r_io
    lane = lax.broadcasted_iota(jnp.int32, (1, LANES), 1)
    own = pl.multiple_of(i * blk, blk)

    for hp in range(ATT_WIDTH // LANES):
        sl = slice(hp * LANES, (hp + 1) * LANES)
        q2 = q_ref[:, sl]
        km = kmean[hp]
        o_pair = None
        for hl in range(LANES // ATT_HEAD_DIM):
            head = hp * (LANES // ATT_HEAD_DIM) + hl
            slope = float(2.0 ** (-8.0 * (head + 1) / ATT_HEADS))
            inhead = (lane // ATT_HEAD_DIM) == hl
            qh = jnp.where(inhead, q2, jnp.zeros_like(q2))
            gate_t = _dot_nt(km, qh.astype(F32), _HI)
            sel = _topk_mask(gate_t, i, k_sel)
            sel = jnp.concatenate([sel, jnp.zeros((LANES - nb, blk), F32)], axis=0)
            selq = sel.T

            s = _dot_nt(qh, kb[pl.ds(own, blk), sl]) * scale + slope * rel
            s = jnp.where(causal, s, NEG)
            m = jnp.max(s, axis=-1, keepdims=True)
            p = jnp.exp(s - m)
            l = jnp.sum(p, axis=-1, keepdims=True)
            acc = _dot(p.astype(BF16), vb[pl.ds(own, blk), sl])

            def past(j, carry, qh=qh, selq=selq, slope=slope, sl=sl):
                m, l, acc = carry
                off = pl.multiple_of(j * blk, blk)
                s = (_dot_nt(qh, kb[pl.ds(off, blk), sl]) * scale
                     + slope * (rel + ((j - i) * blk).astype(F32)))
                chosen = jnp.sum(jnp.where(lane == j, selq, 0.0), axis=-1, keepdims=True)
                s = jnp.where(chosen > 0.5, s, NEG)
                m_new = jnp.maximum(m, jnp.max(s, axis=-1, keepdims=True))
                a = jnp.exp(m - m_new)
                p = jnp.exp(s - m_new)
                l = a * l + jnp.sum(p, axis=-1, keepdims=True)
                acc = a * acc + _dot(p.astype(BF16), vb[pl.ds(off, blk), sl])
                return m_new, l, acc

            m, l, acc = lax.fori_loop(0, i, past, (m, l, acc))
            o_h = acc / l
            o_pair = o_h if hl == 0 else jnp.where(inhead, o_h, o_pair)
        o_ref[:, sl] = o_pair.astype(o_ref.dtype)


def _moba(q_all, k_all, v_all, batch, seq):
    nb = seq // MOBA_BLOCK
    assert seq % MOBA_BLOCK == 0 and nb <= LANES
    return pl.pallas_call(
        functools.partial(_moba_body, nb),
        grid=(batch, nb),
        in_specs=[pl.BlockSpec((MOBA_BLOCK, ATT_WIDTH), lambda b, i: (b * nb + i, 0)),
                  pl.BlockSpec((seq, ATT_WIDTH), lambda b, i: (b, 0)),
                  pl.BlockSpec((seq, ATT_WIDTH), lambda b, i: (b, 0))],
        out_specs=pl.BlockSpec((MOBA_BLOCK, ATT_WIDTH), lambda b, i: (b * nb + i, 0)),
        out_shape=jax.ShapeDtypeStruct((batch * seq, ATT_WIDTH), BF16),
        scratch_shapes=[pltpu.VMEM((seq, ATT_WIDTH), BF16), pltpu.VMEM((seq, ATT_WIDTH), BF16),
                        pltpu.VMEM((ATT_WIDTH // LANES, nb, LANES), F32)],
        compiler_params=_params(("arbitrary", "arbitrary")),
        name="moba",
    )(q_all, k_all, v_all)


def _samp_attn_body(n_blk, past_len, dec_seq, pt_ref, q_ref, kn_ref, vn_ref,
                    k0_ref, k1_ref, v0_ref, v1_ref, o_ref, m_scr, l_scr, acc_scr, g_scr):
    del pt_ref
    j = pl.program_id(1)
    hq = ATT_HEADS * SUBLANES
    scale = ATT_HEAD_DIM ** -0.5
    keys = ATT_HEADS * PAGE_SIZE
    q = q_ref[0]
    rows = lax.broadcasted_iota(jnp.int32, (hq, PAGE_SIZE), 0)
    lanes = lax.broadcasted_iota(jnp.int32, (hq, PAGE_SIZE), 1)
    slope = jnp.exp2((rows // SUBLANES + 1).astype(F32) * (-8.0 / ATT_HEADS))
    tq = (past_len + rows % SUBLANES).astype(F32)
    rows_w = lax.broadcasted_iota(jnp.int32, (hq, keys), 0)
    lanes_w = lax.broadcasted_iota(jnp.int32, (hq, keys), 1)
    diag = (lanes_w // PAGE_SIZE) == (rows_w // SUBLANES)

    ksum = jnp.zeros((ATT_HEADS, ATT_HEAD_DIM), F32)
    parts = []
    for pg, (kr, vr) in enumerate(((k0_ref, v0_ref), (k1_ref, v1_ref))):
        kpage = kr[...]
        ksum = ksum + jnp.sum(kpage, axis=1)
        s_all = _dot_nt(q, kpage.reshape(keys, ATT_HEAD_DIM).astype(BF16))
        s = jnp.concatenate([s_all[h * SUBLANES:(h + 1) * SUBLANES, h * PAGE_SIZE:(h + 1) * PAGE_SIZE]
                             for h in range(ATT_HEADS)], axis=0)
        pos = (j * MOBA_BLOCK + pg * PAGE_SIZE + lanes).astype(F32)
        s = s * scale - slope * (tq - pos)
        m = jnp.max(s, axis=-1, keepdims=True)
        p = jnp.exp(s - m)
        l = jnp.sum(p, axis=-1, keepdims=True)
        p_wide = jnp.where(diag, jnp.concatenate([p] * ATT_HEADS, axis=1), 0.0).astype(BF16)
        acc = _dot(p_wide, vr[...].reshape(keys, ATT_HEAD_DIM).astype(BF16))
        parts.append((m, l, acc))
    (m0, l0, a0), (m1, l1, a1) = parts
    mm = jnp.maximum(m0, m1)
    e0 = jnp.exp(m0 - mm)
    e1 = jnp.exp(m1 - mm)
    m_scr[j] = jnp.broadcast_to(mm, (hq, LANES))
    l_scr[j] = jnp.broadcast_to(e0 * l0 + e1 * l1, (hq, LANES))
    acc_scr[j] = e0 * a0 + e1 * a1

    kmean = ksum * (1.0 / MOBA_BLOCK)
    g_all = _dot_nt(kmean, q.astype(F32), _HI)
    hrow = lax.broadcasted_iota(jnp.int32, (ATT_HEADS, hq), 0)
    hcol = lax.broadcasted_iota(jnp.int32, (ATT_HEADS, hq), 1) // SUBLANES
    g_row = jnp.sum(jnp.where(hrow == hcol, g_all, 0.0), axis=0, keepdims=True)
    g_scr[pl.ds(j, 1), :] = jnp.concatenate([g_row, jnp.zeros((1, LANES - hq), F32)], axis=1)

    @pl.when(j == n_blk - 1)
    def _():
        k_sel = min(MOBA_TOPK, n_blk)
        sel = _topk_mask(g_scr[...], n_blk, k_sel)
        sel = jnp.concatenate([sel, jnp.zeros((LANES - n_blk, LANES), F32)], axis=0)
        selq = sel.T[0:hq, :]

        r2 = lax.broadcasted_iota(jnp.int32, (hq, hq), 0)
        c2 = lax.broadcasted_iota(jnp.int32, (hq, hq), 1)
        ok = ((c2 // SUBLANES) == (r2 // SUBLANES)) & ((c2 % SUBLANES) <= (r2 % SUBLANES)) \
            & ((c2 % SUBLANES) < dec_seq)
        slope2 = jnp.exp2((r2 // SUBLANES + 1).astype(F32) * (-8.0 / ATT_HEADS))
        s = _dot_nt(q, kn_ref[0]) * scale - slope2 * (r2 % SUBLANES - c2 % SUBLANES).astype(F32)
        s = jnp.where(ok, s, NEG)
        m_loc = jnp.max(s, axis=-1, keepdims=True)
        p = jnp.exp(s - m_loc)
        l_loc = jnp.sum(p, axis=-1, keepdims=True)
        a_loc = _dot(p.astype(BF16), vn_ref[0])

        m_tot = m_loc
        for jb in range(n_blk):
            m_tot = jnp.maximum(m_tot, jnp.where(selq[:, jb:jb + 1] > 0.5, m_scr[jb][:, 0:1], NEG))
        w = jnp.exp(m_loc - m_tot)
        num = w * a_loc
        den = w * l_loc
        for jb in range(n_blk):
            w = jnp.where(selq[:, jb:jb + 1] > 0.5, jnp.exp(m_scr[jb][:, 0:1] - m_tot), 0.0)
            num = num + w * acc_scr[jb]
            den = den + w * l_scr[jb][:, 0:1]
        o_ref[0] = num / den


def _samp_attn(q_s, k_s, v_s, cache_k, cache_v, page_table, dec_seq):
    db, n_pages = page_table.shape
    ppb = MOBA_BLOCK // PAGE_SIZE
    assert ppb == 2 and n_pages % ppb == 0 and dec_seq <= SUBLANES
    n_blk = n_pages // ppb
    assert 1 <= n_blk <= LANES
    past_len = n_pages * PAGE_SIZE
    hq = ATT_HEADS * SUBLANES

    def rows(t, dt):
        t = t.reshape(db, dec_seq, ATT_HEADS, ATT_HEAD_DIM).transpose(0, 2, 1, 3)
        t = jnp.pad(t, ((0, 0), (0, 0), (0, SUBLANES - dec_seq), (0, 0)))
        return t.reshape(db, hq, ATT_HEAD_DIM).astype(dt)

    small = pl.BlockSpec((1, hq, ATT_HEAD_DIM), lambda b, j, pt: (b, 0, 0))

    def page(pg):
        return pl.BlockSpec((None, ATT_HEADS, PAGE_SIZE, ATT_HEAD_DIM),
                            lambda b, j, pt: (pt[b * n_pages + ppb * j + pg], 0, 0, 0))

    out = pl.pallas_call(
        functools.partial(_samp_attn_body, n_blk, past_len, dec_seq),
        grid_spec=pltpu.PrefetchScalarGridSpec(
            num_scalar_prefetch=1,
            grid=(db, n_blk),
            in_specs=[small, small, small, page(0), page(1), page(0), page(1)],
            out_specs=small,
            scratch_shapes=[pltpu.VMEM((n_blk, hq, LANES), F32), pltpu.VMEM((n_blk, hq, LANES), F32),
                            pltpu.VMEM((n_blk, hq, ATT_HEAD_DIM), F32), pltpu.VMEM((n_blk, LANES), F32)]),
        out_shape=jax.ShapeDtypeStruct((db, hq, ATT_HEAD_DIM), F32),
        compiler_params=_params(("arbitrary", "arbitrary")),
        name="samp_attn",
    )(page_table.reshape(-1), rows(q_s, BF16), rows(k_s, BF16), rows(v_s, BF16),
      cache_k, cache_k, cache_v, cache_v)
    out = out.reshape(db, ATT_HEADS, SUBLANES, ATT_HEAD_DIM)[:, :, :dec_seq]
    return out.transpose(0, 2, 1, 3).reshape(db * dec_seq, ATT_WIDTH)


def _mlstm_body(l_valid, qk_ref, v_ref, o_ref, g_ref, conv0_ref, c0_ref, n0_ref, m0_ref,
                cw_ref, cb_ref, gb_ref, ln_ref,
                hn_ref, convn_ref, c_ref, n_ref, m_ref, ubuf):
    c = pl.program_id(1)
    lc = LSTM_CHUNK
    hd = LSTM_HEAD_DIM

    @pl.when(c == 0)
    def _():
        ubuf[0:SUBLANES, :] = conv0_ref[...]
        c_ref[...] = c0_ref[...]
        n_ref[...] = n0_ref[...]
        m_ref[...] = m0_ref[...]

    u = qk_ref[...]
    ubuf[SUBLANES:SUBLANES + lc, :] = u
    y = cb_ref[...] + u * cw_ref[CONV_WIDTH - 1:CONV_WIDTH, :]
    for s in range(1, CONV_WIDTH):
        y = y + ubuf[SUBLANES - s:SUBLANES - s + lc, :] * cw_ref[CONV_WIDTH - 1 - s:CONV_WIDTH - s, :]
    qk = y * jax.nn.sigmoid(y)
    convn_ref[0] = ubuf[SUBLANES + l_valid - (CONV_WIDTH - 1):SUBLANES + l_valid, :]
    ubuf[0:SUBLANES, :] = ubuf[lc:lc + SUBLANES, :]

    row = lax.broadcasted_iota(jnp.int32, (lc, lc), 0)
    col = lax.broadcasted_iota(jnp.int32, (lc, lc), 1)
    tri = col <= row
    g = g_ref[...] + gb_ref[...]
    log_f = jnp.minimum(g, 0.0) - jnp.log1p(jnp.exp(-jnp.abs(g)))
    is_f = (col >= LSTM_HEADS) & (col < 2 * LSTM_HEADS)
    gates = jnp.where(is_f, log_f, g)
    if l_valid < lc:
        gates = jnp.where(row < l_valid, gates, jnp.where(is_f, 0.0, NEG))
    csum = jnp.dot(jnp.where(tri, 1.0, 0.0), jnp.where(is_f, gates, 0.0),
                   precision=_HI, preferred_element_type=F32)
    gates_t = gates.T
    csum_t = csum.T

    for h in range(LSTM_HEADS):
        hs = slice(h * hd, (h + 1) * hd)
        qf = qk[:, hs]
        kf = qk[:, LSTM_WIDTH + h * hd:LSTM_WIDTH + (h + 1) * hd] * (hd ** -0.5)
        vf = v_ref[:, hs]
        qb, kb, vb = qf.astype(BF16), kf.astype(BF16), vf.astype(BF16)
        b_col = csum[:, LSTM_HEADS + h:LSTM_HEADS + h + 1]
        b_row = csum_t[LSTM_HEADS + h:LSTM_HEADS + h + 1, :]
        ig_row = gates_t[h:h + 1, :]
        ig_col = gates[:, h:h + 1]
        m0 = m_ref[0, h:h + 1, 0:1]
        n0 = n_ref[0, h:h + 1, :]
        c0 = c_ref[0, h]

        log_d = jnp.where(tri, b_col - b_row + ig_row, NEG)
        dec0 = b_col + m0
        m_t = jnp.maximum(dec0, jnp.max(log_d, axis=-1, keepdims=True))
        d_m = jnp.exp(log_d - m_t)
        w0 = jnp.exp(dec0 - m_t)
        s = _dot_nt(qb, kb) * d_m
        num = w0 * _dot_nt(qb, c0.astype(BF16)) + _dot(s.astype(BF16), vb)
        den = w0 * jnp.sum(qf * n0, axis=-1, keepdims=True) + jnp.sum(s, axis=-1, keepdims=True)
        hh = num / jnp.maximum(jnp.abs(den), jnp.exp(-m_t))

        m_l = m_t[lc - 1:lc, :]
        b_l = b_col[lc - 1:lc, :]
        w_l = jnp.exp(b_l - b_col + ig_col - m_l)
        w0_l = jnp.exp(b_l + m0 - m_l)
        c_ref[0, h] = w0_l * c0 + _dot((w_l * vf).T.astype(BF16), kb)
        n_ref[0, h:h + 1, :] = w0_l * n0 + jnp.sum(w_l * kf, axis=0, keepdims=True)
        m_ref[0, h:h + 1, :] = jnp.broadcast_to(m_l, (1, LANES))

        hn = hh * lax.rsqrt(jnp.mean(hh * hh, axis=-1, keepdims=True) + NORM_EPS)
        hn = hn * ln_ref[:, hs] * jax.nn.sigmoid(o_ref[:, hs])
        hn_ref[:, hs] = hn.astype(hn_ref.dtype)


def _mlstm(qk, v, o, gates, conv0, c0, n0, m0, conv_w, conv_b, gate_b, lstm_norm, batch, n_chunks, l_valid):
    assert CONV_WIDTH - 1 <= l_valid <= LSTM_CHUNK
    nc = n_chunks
    tok = lambda b, c: (b * nc + c, 0)
    per_b3 = lambda b, c: (b, 0, 0)
    const = lambda b, c: (0, 0)
    rows = batch * nc * LSTM_CHUNK
    return pl.pallas_call(
        functools.partial(_mlstm_body, l_valid),
        grid=(batch, nc),
        in_specs=[pl.BlockSpec((LSTM_CHUNK, 2 * LSTM_WIDTH), tok),
                  pl.BlockSpec((LSTM_CHUNK, LSTM_WIDTH), tok),
                  pl.BlockSpec((LSTM_CHUNK, LSTM_WIDTH), tok),
                  pl.BlockSpec((LSTM_CHUNK, LANES), tok),
                  pl.BlockSpec((SUBLANES, 2 * LSTM_WIDTH), lambda b, c: (b, 0)),
                  pl.BlockSpec((1, LSTM_HEADS, LSTM_HEAD_DIM, LSTM_HEAD_DIM), lambda b, c: (b, 0, 0, 0)),
                  pl.BlockSpec((1, SUBLANES, LANES), per_b3),
                  pl.BlockSpec((1, SUBLANES, LANES), per_b3),
                  pl.BlockSpec((CONV_WIDTH, 2 * LSTM_WIDTH), const),
                  pl.BlockSpec((1, 2 * LSTM_WIDTH), const),
                  pl.BlockSpec((1, LANES), const),
                  pl.BlockSpec((1, LSTM_WIDTH), const)],
        out_specs=[pl.BlockSpec((LSTM_CHUNK, LSTM_WIDTH), tok),
                   pl.BlockSpec((1, CONV_WIDTH - 1, 2 * LSTM_WIDTH), per_b3),
                   pl.BlockSpec((1, LSTM_HEADS, LSTM_HEAD_DIM, LSTM_HEAD_DIM), lambda b, c: (b, 0, 0, 0)),
                   pl.BlockSpec((1, SUBLANES, LANES), per_b3),
                   pl.BlockSpec((1, SUBLANES, LANES), per_b3)],
        out_shape=[jax.ShapeDtypeStruct((rows, LSTM_WIDTH), BF16),
                   jax.ShapeDtypeStruct((batch, CONV_WIDTH - 1, 2 * LSTM_WIDTH), F32),
                   jax.ShapeDtypeStruct((batch, LSTM_HEADS, LSTM_HEAD_DIM, LSTM_HEAD_DIM), F32),
                   jax.ShapeDtypeStruct((batch, SUBLANES, LANES), F32),
                   jax.ShapeDtypeStruct((batch, SUBLANES, LANES), F32)],
        scratch_shapes=[pltpu.VMEM((LSTM_CHUNK + 2 * SUBLANES, 2 * LSTM_WIDTH), F32)],
        compiler_params=_params(("arbitrary", "arbitrary")),
        name="mlstm",
    )(qk, v, o, gates, conv0, c0, n0, m0, conv_w, conv_b, gate_b, lstm_norm)


def _back_body(n_p, ap_ref, as_ref, hp_ref, hs_ref, xp_ref, xs_ref, woa_ref, wob_ref, g_ref, wr_ref, br_ref,
               x1_ref, h2_ref, rt_ref):
    i = pl.program_id(0)
    is_p = i < n_p
    att = jnp.where(is_p, ap_ref[...], as_ref[...])
    hn = jnp.where(is_p, hp_ref[...], hs_ref[...])
    x = jnp.where(is_p, xp_ref[...], xs_ref[...])
    x1 = x + _dot(att, woa_ref[...]) + _dot(hn, wob_ref[...])
    x1_ref[...] = x1
    ms = jnp.mean(x1 * x1, axis=-1, keepdims=True)
    h2 = (x1 * lax.rsqrt(ms + NORM_EPS) * g_ref[...]).astype(BF16)
    h2_ref[...] = h2
    logits = _dot(h2, wr_ref[...]) + br_ref[...]
    lane = lax.broadcasted_iota(jnp.int32, logits.shape, 1).astype(F32)
    cur = logits
    vals, ids = [], []
    for _ in range(TOP_K):
        mk = jnp.max(cur, axis=-1, keepdims=True)
        ik = jnp.min(jnp.where(cur == mk, lane, float(LANES)), axis=-1, keepdims=True)
        cur = jnp.where(lane == ik, -3e38, cur)
        vals.append(mk)
        ids.append(ik)
    es = [jnp.exp(v - vals[0]) for v in vals]
    tot = es[0]
    for e in es[1:]:
        tot = tot + e
    out = jnp.zeros(logits.shape, F32)
    for k in range(TOP_K):
        out = jnp.where(lane == float(k), ids[k], out)
        out = jnp.where(lane == float(TOP_K + k), es[k] / tot, out)
    rt_ref[...] = out


def _back(att_p, att_s, hn_p, hn_s, xp, xs, wo_a, wo_b, g, w_r, b_r):
    t_p, t_s = xp.shape[0], xs.shape[0]
    tm = t_s
    n_p = t_p // tm
    t_all = t_p + t_s
    row = lambda i: (i, 0)
    prow = lambda i: (jnp.minimum(i, n_p - 1), 0)
    const = lambda i: (0, 0)
    return pl.pallas_call(
        functools.partial(_back_body, n_p),
        grid=(n_p + 1,),
        in_specs=[pl.BlockSpec((tm, ATT_WIDTH), prow), pl.BlockSpec((tm, ATT_WIDTH), const),
                  pl.BlockSpec((tm, LSTM_WIDTH), prow), pl.BlockSpec((tm, LSTM_WIDTH), const),
                  pl.BlockSpec((tm, D_MODEL), prow), pl.BlockSpec((tm, D_MODEL), const),
                  pl.BlockSpec((ATT_WIDTH, D_MODEL), const), pl.BlockSpec((LSTM_WIDTH, D_MODEL), const),
                  pl.BlockSpec((1, D_MODEL), const), pl.BlockSpec((D_MODEL, LANES), const),
                  pl.BlockSpec((1, LANES), const)],
        out_specs=[pl.BlockSpec((tm, D_MODEL), row), pl.BlockSpec((tm, D_MODEL), row),
                   pl.BlockSpec((tm, LANES), row)],
        out_shape=[jax.ShapeDtypeStruct((t_all, D_MODEL), F32), jax.ShapeDtypeStruct((t_all, D_MODEL), BF16),
                   jax.ShapeDtypeStruct((t_all, LANES), F32)],
        compiler_params=_params(("arbitrary",)),
        name="back",
    )(att_p, att_s, hn_p, hn_s, xp, xs, wo_a, wo_b, g, w_r, b_r)


def _expert_body(be_ref, nu_ref, x_ref, wu_ref, bu_ref, wd_ref, bd_ref, y_ref):
    del be_ref

    @pl.when(pl.program_id(0) < nu_ref[0])
    def _():
        xb = x_ref[...]
        acc = None
        for c in range(D_FF // COL_CHUNK):
            lo, hi = c * COL_CHUNK, (c + 1) * COL_CHUNK
            glu = _dot(xb, wu_ref[0, :, lo:hi]) + bu_ref[0, :, lo:hi]
            lin = _dot(xb, wu_ref[0, :, D_FF + lo:D_FF + hi]) + bu_ref[0, :, D_FF + lo:D_FF + hi]
            glu = jnp.minimum(glu, SWIGLU_LIMIT)
            lin = jnp.clip(lin, -SWIGLU_LIMIT, SWIGLU_LIMIT)
            mid = (glu * jax.nn.sigmoid(SWIGLU_ALPHA * glu) * (lin + 1.0)).astype(BF16)
            part = _dot(mid, wd_ref[0, lo:hi, :])
            acc = part if acc is None else acc + part
        y_ref[...] = acc + bd_ref[0]


def _experts(xs, blk_expert, n_used, w_up, b_up, w_down, b_down):
    cap = xs.shape[0]
    n_blocks = cap // EXPERT_ROWS
    live = lambda i, be, nu: jnp.minimum(i, nu[0] - 1)
    return pl.pallas_call(
        _expert_body,
        grid_spec=pltpu.PrefetchScalarGridSpec(
            num_scalar_prefetch=2,
            grid=(n_blocks,),
            in_specs=[pl.BlockSpec((EXPERT_ROWS, D_MODEL), lambda i, be, nu: (live(i, be, nu), 0)),
                      pl.BlockSpec((1, D_MODEL, 2 * D_FF), lambda i, be, nu: (be[live(i, be, nu)], 0, 0)),
                      pl.BlockSpec((1, 1, 2 * D_FF), lambda i, be, nu: (be[live(i, be, nu)], 0, 0)),
                      pl.BlockSpec((1, D_FF, D_MODEL), lambda i, be, nu: (be[live(i, be, nu)], 0, 0)),
                      pl.BlockSpec((1, 1, D_MODEL), lambda i, be, nu: (be[live(i, be, nu)], 0, 0))],
            out_specs=pl.BlockSpec((EXPERT_ROWS, D_MODEL), lambda i, be, nu: (live(i, be, nu), 0))),
        out_shape=jax.ShapeDtypeStruct((cap, D_MODEL), F32),
        compiler_params=_params(("arbitrary",)),
        name="experts",
    )(blk_expert, n_used, xs, w_up, b_up, w_down, b_down)


def _combine_body(n_p, x1_ref, yg_ref, rt_ref, g_ref, yp_ref, ys_ref):
    i = pl.program_id(0)
    y = x1_ref[...]
    rt = rt_ref[...]
    for k in range(TOP_K):
        y = y + yg_ref[:, k * D_MODEL:(k + 1) * D_MODEL] * rt[:, TOP_K + k:TOP_K + k + 1]
    ms = jnp.mean(y * y, axis=-1, keepdims=True)
    out = y * lax.rsqrt(ms + NORM_EPS) * g_ref[...]

    @pl.when(i < n_p)
    def _():
        yp_ref[...] = out

    @pl.when(i >= n_p)
    def _():
        ys_ref[...] = out


def _combine(x1, yg, rt, g, t_p, t_s):
    tm = min(256, t_s)
    assert t_p % tm == 0 and t_s % tm == 0
    n_p, n_s = t_p // tm, t_s // tm
    row = lambda i: (i, 0)
    return pl.pallas_call(
        functools.partial(_combine_body, n_p),
        grid=(n_p + n_s,),
        in_specs=[pl.BlockSpec((tm, D_MODEL), row), pl.BlockSpec((tm, TOP_K * D_MODEL), row),
                  pl.BlockSpec((tm, LANES), row), pl.BlockSpec((1, D_MODEL), lambda i: (0, 0))],
        out_specs=[pl.BlockSpec((tm, D_MODEL), lambda i: (jnp.minimum(i, n_p - 1), 0)),
                   pl.BlockSpec((tm, D_MODEL), lambda i: (jnp.maximum(i - n_p, 0), 0))],
        out_shape=[jax.ShapeDtypeStruct((t_p, D_MODEL), F32), jax.ShapeDtypeStruct((t_s, D_MODEL), F32)],
        compiler_params=_params(("arbitrary",)),
        name="combine",
    )(x1, yg, rt, g)


def _route(top_idx):
    t = top_idx.shape[0]
    n = t * TOP_K
    bm = EXPERT_ROWS
    e_flat = top_idx.reshape(-1)
    e_sorted, order = lax.sort((e_flat, jnp.arange(n, dtype=jnp.int32)), num_keys=1, is_stable=True)
    experts = jnp.arange(N_EXPERTS, dtype=jnp.int32)
    counts = jnp.sum((e_flat[:, None] == experts[None, :]).astype(jnp.int32), axis=0)
    padded = ((counts + bm - 1) // bm) * bm
    pad_end = jnp.cumsum(padded)
    pad_start = pad_end - padded
    grp_start = jnp.cumsum(counts) - counts
    n_blocks = -(-n // bm) + N_EXPERTS
    blk_start = jnp.arange(n_blocks, dtype=jnp.int32) * bm
    blk_expert = jnp.minimum(jnp.searchsorted(pad_end, blk_start, side="right"), N_EXPERTS - 1).astype(jnp.int32)
    n_used = (pad_end[-1] // bm).astype(jnp.int32).reshape(1)
    shift = (pad_start - grp_start).astype(jnp.int32)
    src = jnp.clip(blk_start - shift[blk_expert], 0, n)
    order_pad = jnp.concatenate([order, jnp.zeros((bm,), jnp.int32)])
    row_flat = jax.vmap(lambda s: lax.dynamic_slice(order_pad, (s,), (bm,)))(src).reshape(-1)
    row_tok = row_flat // TOP_K
    shift_sorted = jnp.sum(jnp.where(e_sorted[:, None] == experts[None, :], shift[None, :], 0), axis=1)
    dest_sorted = jnp.arange(n, dtype=jnp.int32) + shift_sorted
    _, dest = lax.sort((order, dest_sorted), num_keys=1)
    return row_tok, blk_expert, n_used, dest


def kernel(x_prompt, x_sample, cache_k, cache_v, page_table, state_conv, state_C, state_n, state_m,
           norm_mix, w_in, conv_w, conv_b, gate_b, lstm_norm, w_out, norm_ffn, w_router, b_router,
           w_up, b_up, w_down, b_down, norm_final):
    assert w_in.shape[0] == 1, "single-layer stack"
    batch, seq, _ = x_prompt.shape
    db, ds, _ = x_sample.shape
    t_p, t_s = batch * seq, db * ds
    n_pool = cache_k.shape[1]
    assert seq % LSTM_CHUNK == 0 and page_table.shape[1] * PAGE_SIZE % MOBA_BLOCK == 0

    xp = x_prompt.reshape(t_p, D_MODEL)
    xs = x_sample.reshape(t_s, D_MODEL)
    w_main = w_in[0, :, :MAIN_COLS].astype(BF16)
    w_gate = jnp.pad(w_in[0, :, MAIN_COLS:], ((0, 0), (0, LANES - 2 * LSTM_HEADS))).astype(BF16)
    q_a, k_a, v_a, qk_src, v_l, o_l, gts = _front(xp, xs, norm_mix, w_main, w_gate)

    att_p = _moba(q_a, k_a, v_a, batch, seq)
    k_s, v_s = k_a[t_p:], v_a[t_p:]
    att_s = _samp_attn(q_a[t_p:], k_s, v_s,
                       cache_k.reshape(n_pool, ATT_HEADS, PAGE_SIZE, ATT_HEAD_DIM),
                       cache_v.reshape(n_pool, ATT_HEADS, PAGE_SIZE, ATT_HEAD_DIM), page_table, ds).astype(BF16)

    gate_b_row = jnp.pad(gate_b, ((0, 0), (0, LANES - 2 * LSTM_HEADS)))
    cw, cb = conv_w[0], conv_b
    hn_p, conv_p, c_p, n_p_, m_p = _mlstm(
        qk_src, v_l, o_l, gts,
        jnp.zeros((batch * SUBLANES, 2 * LSTM_WIDTH), F32),
        jnp.zeros((batch, LSTM_HEADS, LSTM_HEAD_DIM, LSTM_HEAD_DIM), F32),
        jnp.zeros((batch, SUBLANES, LANES), F32), jnp.zeros((batch, SUBLANES, LANES), F32),
        cw, cb, gate_b_row, lstm_norm, batch, seq // LSTM_CHUNK, LSTM_CHUNK)

    def pad_chunk(a):
        a = a[t_p:].reshape(db, ds, a.shape[-1])
        return jnp.pad(a, ((0, 0), (0, LSTM_CHUNK - ds), (0, 0))).reshape(db * LSTM_CHUNK, a.shape[-1])

    conv0_s = jnp.pad(state_conv[0], ((0, 0), (SUBLANES - (CONV_WIDTH - 1), 0), (0, 0)))
    n0_s = jnp.pad(state_n[0], ((0, 0), (0, SUBLANES - LSTM_HEADS), (0, 0)))
    m0_s = jnp.broadcast_to(jnp.pad(state_m[0], ((0, 0), (0, SUBLANES - LSTM_HEADS)))[:, :, None],
                            (db, SUBLANES, LANES))
    hn_s, conv_s, c_s, n_s, m_s = _mlstm(
        pad_chunk(qk_src), pad_chunk(v_l), pad_chunk(o_l), pad_chunk(gts),
        conv0_s.reshape(db * SUBLANES, 2 * LSTM_WIDTH), state_C[0], n0_s, m0_s,
        cw, cb, gate_b_row, lstm_norm, db, 1, ds)
    hn_s = hn_s.reshape(db, LSTM_CHUNK, LSTM_WIDTH)[:, :ds].reshape(t_s, LSTM_WIDTH)

    wo = w_out[0].astype(BF16)
    w_r = jnp.pad(w_router[0], ((0, 0), (0, LANES - N_EXPERTS))).astype(BF16)
    b_r = jnp.pad(b_router, ((0, 0), (0, LANES - N_EXPERTS)), constant_values=NEG)
    x1, h2, rt = _back(att_p, att_s, hn_p, hn_s, xp, xs, wo[:ATT_WIDTH], wo[ATT_WIDTH:], norm_ffn, w_r, b_r)

    row_tok, blk_expert, n_used, dest = _route(rt[:, :TOP_K].astype(jnp.int32))
    ys = _experts(h2[row_tok], blk_expert, n_used, w_up[0].astype(BF16), b_up[0][:, None, :],
                  w_down[0].astype(BF16), b_down[0][:, None, :])
    yg = ys[dest].reshape(t_p + t_s, TOP_K * D_MODEL)
    y_p, y_s = _combine(x1, yg, rt, norm_final.reshape(1, D_MODEL), t_p, t_s)

    def to_pages(t):
        t = t[:t_p].reshape(batch, seq // PAGE_SIZE, PAGE_SIZE, ATT_HEADS, ATT_HEAD_DIM)
        return t.transpose(0, 1, 3, 2, 4)[None]

    def to_slots(t):
        return t.reshape(db, ds, ATT_HEADS, ATT_HEAD_DIM).transpose(0, 2, 1, 3)[None]

    return (y_p.reshape(batch, seq, D_MODEL), y_s.reshape(db, ds, D_MODEL),
            to_pages(k_a), to_pages(v_a), to_slots(k_s), to_slots(v_s),
            conv_p[None], conv_s[None],
            c_p[None], n_p_[None, :, :LSTM_HEADS, :], m_p[None, :, :LSTM_HEADS, 0],
            c_s[None], n_s[None, :, :LSTM_HEADS, :], m_s[None, :, :LSTM_HEADS, 0])
```

```python
import functools

import jax
import jax.numpy as jnp
from jax import lax
from jax.experimental import pallas as pl
from jax.experimental.pallas import tpu as pltpu

F32 = jnp.float32
BF16 = jnp.bfloat16

D_MODEL = 1024
ATT_HEADS = 8
ATT_HEAD_DIM = 64
ATT_WIDTH = ATT_HEADS * ATT_HEAD_DIM
MOBA_BLOCK = 256
MOBA_TOPK = 3
LSTM_HEADS = 4
LSTM_HEAD_DIM = 128
LSTM_WIDTH = LSTM_HEADS * LSTM_HEAD_DIM
CONV_WIDTH = 4
LSTM_CHUNK = 128
PAGE_SIZE = 128
N_EXPERTS = 32
TOP_K = 4
D_FF = D_MODEL
SWIGLU_LIMIT = 7.0
SWIGLU_ALPHA = 1.702
NORM_EPS = 1e-5
MAIN_COLS = 3 * ATT_WIDTH + 4 * LSTM_WIDTH
COL_CHUNK = 512
LANES = 128
SUBLANES = 8
EXPERT_ROWS = 512
NEG = -1e30
LOG2E = 1.4426950408889634
VMEM_LIMIT = 48 * 1024 * 1024

_NT = (((1,), (1,)), ((), ()))
_HI = lax.Precision.HIGHEST


def _dot(a, b):
    return jnp.dot(a, b, preferred_element_type=F32)


def _dot_nt(a, b, precision=None):
    return lax.dot_general(a, b, _NT, precision=precision, preferred_element_type=F32)


def _params(sem):
    return pltpu.CompilerParams(dimension_semantics=sem, vmem_limit_bytes=VMEM_LIMIT)


def _front_body(n_p, xp_ref, xs_ref, g_ref, w_ref, wg_ref,
                qa_ref, ka_ref, va_ref, qk_ref, vl_ref, ol_ref, gt_ref):
    i = pl.program_id(0)
    x = jnp.where(i < n_p, xp_ref[...], xs_ref[...])
    ms = jnp.mean(x * x, axis=-1, keepdims=True)
    h = (x * lax.rsqrt(ms + NORM_EPS) * g_ref[...]).astype(BF16)
    dests = ((qa_ref, 0), (ka_ref, 0), (va_ref, 0), (qk_ref, 0), (qk_ref, 1), (vl_ref, 0), (ol_ref, 0))
    for c, (ref, part) in enumerate(dests):
        z = _dot(h, w_ref[:, c * COL_CHUNK:(c + 1) * COL_CHUNK])
        ref[:, part * COL_CHUNK:(part + 1) * COL_CHUNK] = z.astype(ref.dtype)
    gt_ref[...] = _dot(h, wg_ref[...])


def _front(xp, xs, g, w_main, w_gate):
    t_p, t_s = xp.shape[0], xs.shape[0]
    tm = t_s
    assert t_p % tm == 0 and tm % 128 == 0
    n_p = t_p // tm
    t_all = t_p + t_s
    row = lambda i: (i, 0)
    const = lambda i: (0, 0)
    wide = lambda n, dt: jax.ShapeDtypeStruct((t_all, n), dt)
    return pl.pallas_call(
        functools.partial(_front_body, n_p),
        grid=(n_p + 1,),
        in_specs=[pl.BlockSpec((tm, D_MODEL), lambda i: (jnp.minimum(i, n_p - 1), 0)),
                  pl.BlockSpec((tm, D_MODEL), const),
                  pl.BlockSpec((1, D_MODEL), const),
                  pl.BlockSpec((D_MODEL, MAIN_COLS), const),
                  pl.BlockSpec((D_MODEL, LANES), const)],
        out_specs=[pl.BlockSpec((tm, ATT_WIDTH), row), pl.BlockSpec((tm, ATT_WIDTH), row),
                   pl.BlockSpec((tm, ATT_WIDTH), row), pl.BlockSpec((tm, 2 * LSTM_WIDTH), row),
                   pl.BlockSpec((tm, LSTM_WIDTH), row), pl.BlockSpec((tm, LSTM_WIDTH), row),
                   pl.BlockSpec((tm, LANES), row)],
        out_shape=[wide(ATT_WIDTH, BF16), wide(ATT_WIDTH, F32), wide(ATT_WIDTH, F32),
                   wide(2 * LSTM_WIDTH, F32), wide(LSTM_WIDTH, F32), wide(LSTM_WIDTH, F32),
                   wide(LANES, F32)],
        compiler_params=_params(("arbitrary",)),
        name="front",
    )(xp, xs, g, w_main, w_gate)


def _topk_rows(g, n_valid, k_sel):
    n = g.shape[0]
    jj = lax.broadcasted_iota(jnp.int32, g.shape, 0)
    rank = jnp.zeros(g.shape, F32)
    for j2 in range(n):
        row = g[j2:j2 + 1, :]
        beats = jnp.where(row > g, 1.0, jnp.where((row == g) & (jj > j2), 1.0, 0.0))
        rank = rank + jnp.where(j2 < n_valid, beats, 0.0)
    return jnp.where((jj < n_valid) & (rank < k_sel), 1.0, 0.0)


def _topk_lanes(g, n, k_sel):
    jj = lax.broadcasted_iota(jnp.int32, g.shape, 1)
    rank = jnp.zeros(g.shape, F32)
    for j2 in range(n):
        col = g[:, j2:j2 + 1]
        rank = rank + jnp.where(col > g, 1.0, jnp.where((col == g) & (jj > j2), 1.0, 0.0))
    return jnp.where((jj < n) & (rank < k_sel), 1.0, 0.0)


def _moba_body(nb, q_ref, k_ref, v_ref, o_ref, kb, vb, kmean):
    i = pl.program_id(1)
    blk = MOBA_BLOCK
    k_sel = min(MOBA_TOPK, nb - 1)
    heads_per = LANES // ATT_HEAD_DIM

    @pl.when(i == 0)
    def _():
        for c in range(ATT_WIDTH // LANES):
            sl = slice(c * LANES, (c + 1) * LANES)
            kf = k_ref[:, sl]
            kb[:, sl] = kf.astype(BF16)
            vb[:, sl] = v_ref[:, sl].astype(BF16)
            kmean[c] = jnp.sum(kf.reshape(nb, blk, LANES), axis=1) * (1.0 / blk)

    r_io = lax.broadcasted_iota(jnp.int32, (blk, blk), 0)
    c_io = lax.broadcasted_iota(jnp.int32, (blk, blk), 1)
    rel = (c_io - r_io).astype(F32)
    causal = c_io <= r_io
    lane = lax.broadcasted_iota(jnp.int32, (1, LANES), 1)
    own = pl.multiple_of(i * blk, blk)

    for hp in range(ATT_WIDTH // LANES):
        sl = slice(hp * LANES, (hp + 1) * LANES)
        q2 = q_ref[:, sl] * (ATT_HEAD_DIM ** -0.5)
        km = kmean[hp]
        k_own = kb[pl.ds(own, blk), sl]
        v_own = vb[pl.ds(own, blk), sl]
        per_head = []
        carry = ()
        for hl in range(heads_per):
            head = hp * heads_per + hl
            slope2 = float(2.0 ** (-8.0 * (head + 1) / ATT_HEADS)) * LOG2E
            inhead = (lane // ATT_HEAD_DIM) == hl
            qh = jnp.where(inhead, q2, jnp.zeros_like(q2))
            gate_t = _dot_nt(km, qh.astype(F32), _HI)
            sel = _topk_rows(gate_t, i, k_sel)
            sel = jnp.concatenate([sel, jnp.zeros((LANES - nb, blk), F32)], axis=0)
            selq = sel.T
            relc = rel * slope2

            t = jnp.where(causal, _dot_nt(qh, k_own) * LOG2E + relc, NEG)
            m = jnp.max(t, axis=-1, keepdims=True)
            p = jnp.exp2(t - m)
            l = jnp.sum(p, axis=-1, keepdims=True)
            acc = _dot(p.astype(BF16), v_own)
            per_head.append((qh, selq, relc, slope2))
            carry = carry + (m, l, acc)

        def past(j, carry, per_head=per_head, sl=sl):
            off = pl.multiple_of(j * blk, blk)
            kj = kb[pl.ds(off, blk), sl]
            vj = vb[pl.ds(off, blk), sl]
            out = ()
            for hl, (qh, selq, relc, slope2) in enumerate(per_head):
                m, l, acc = carry[3 * hl:3 * hl + 3]
                c0 = slope2 * ((j - i) * blk).astype(F32)
                t = _dot_nt(qh, kj) * LOG2E + relc
                chosen = jnp.sum(jnp.where(lane == j, selq, 0.0), axis=-1, keepdims=True) > 0.5
                m_new = jnp.maximum(m, jnp.where(chosen, jnp.max(t, axis=-1, keepdims=True) + c0, NEG))
                a = jnp.exp2(m - m_new)
                p = jnp.exp2(t - jnp.where(chosen, m_new - c0, -NEG))
                l = a * l + jnp.sum(p, axis=-1, keepdims=True)
                acc = a * acc + _dot(p.astype(BF16), vj)
                out = out + (m_new, l, acc)
            return out

        carry = lax.fori_loop(0, i, past, carry)
        o_pair = None
        for hl in range(heads_per):
            m, l, acc = carry[3 * hl:3 * hl + 3]
            o_h = acc / l
            o_pair = o_h if hl == 0 else jnp.where((lane // ATT_HEAD_DIM) == hl, o_h, o_pair)
        o_ref[:, sl] = o_pair.astype(o_ref.dtype)


def _moba(q_all, k_all, v_all, batch, seq):
    nb = seq // MOBA_BLOCK
    assert seq % MOBA_BLOCK == 0 and nb <= LANES
    return pl.pallas_call(
        functools.partial(_moba_body, nb),
        grid=(batch, nb),
        in_specs=[pl.BlockSpec((MOBA_BLOCK, ATT_WIDTH), lambda b, i: (b * nb + i, 0)),
                  pl.BlockSpec((seq, ATT_WIDTH), lambda b, i: (b, 0)),
                  pl.BlockSpec((seq, ATT_WIDTH), lambda b, i: (b, 0))],
        out_specs=pl.BlockSpec((MOBA_BLOCK, ATT_WIDTH), lambda b, i: (b * nb + i, 0)),
        out_shape=jax.ShapeDtypeStruct((batch * seq, ATT_WIDTH), BF16),
        scratch_shapes=[pltpu.VMEM((seq, ATT_WIDTH), BF16), pltpu.VMEM((seq, ATT_WIDTH), BF16),
                        pltpu.VMEM((ATT_WIDTH // LANES, nb, LANES), F32)],
        compiler_params=_params(("arbitrary", "arbitrary")),
        name="moba",
    )(q_all, k_all, v_all)


def _samp_attn_body(n_blk, bps, past_len, dec_seq, pt_ref, qbd_ref, q_ref, kn_ref, vn_ref, *rest):
    del pt_ref
    ppb = MOBA_BLOCK // PAGE_SIZE
    k_refs, v_refs = rest[:ppb * bps], rest[ppb * bps:2 * ppb * bps]
    o_ref, m_scr, l_scr, acc_scr, g_scr = rest[2 * ppb * bps:]
    step = pl.program_id(1)
    hq = ATT_HEADS * SUBLANES
    scale = ATT_HEAD_DIM ** -0.5
    qbd = qbd_ref[0]
    rows = lax.broadcasted_iota(jnp.int32, (hq, PAGE_SIZE), 0)
    lanes = lax.broadcasted_iota(jnp.int32, (hq, PAGE_SIZE), 1)
    slope = jnp.exp2((rows // SUBLANES + 1).astype(F32) * (-8.0 / ATT_HEADS))
    tq = (past_len + rows % SUBLANES).astype(F32)

    @pl.when(step == 0)
    def _():
        g_scr[...] = jnp.zeros(g_scr.shape, F32)

    for bl in range(bps):
        j = step * bps + bl
        parts = []
        gsum = jnp.zeros((hq, 1), F32)
        for pg in range(ppb):
            s_raw = _dot(qbd, k_refs[ppb * bl + pg][...].astype(BF16))
            gsum = gsum + jnp.sum(s_raw, axis=-1, keepdims=True)
            pos = (j * MOBA_BLOCK + pg * PAGE_SIZE + lanes).astype(F32)
            s = s_raw * scale - slope * (tq - pos)
            m = jnp.max(s, axis=-1, keepdims=True)
            p = jnp.exp(s - m)
            l = jnp.sum(p, axis=-1, keepdims=True)
            acc = _dot_nt(p.astype(BF16), v_refs[ppb * bl + pg][...].astype(BF16))
            parts.append((m, l, acc))
        (m0, l0, a0), (m1, l1, a1) = parts
        mm = jnp.maximum(m0, m1)
        e0 = jnp.exp(m0 - mm)
        e1 = jnp.exp(m1 - mm)
        m_scr[j] = jnp.broadcast_to(mm, (hq, LANES))
        l_scr[j] = jnp.broadcast_to(e0 * l0 + e1 * l1, (hq, LANES))
        acc_scr[j] = e0 * a0 + e1 * a1
        g_scr[...] = jnp.where(lanes == j, gsum * (1.0 / MOBA_BLOCK), g_scr[...])

    @pl.when(step == n_blk // bps - 1)
    def _():
        selq = _topk_lanes(g_scr[...], n_blk, min(MOBA_TOPK, n_blk))

        r2 = lax.broadcasted_iota(jnp.int32, (hq, hq), 0)
        c2 = lax.broadcasted_iota(jnp.int32, (hq, hq), 1)
        ok = ((c2 // SUBLANES) == (r2 // SUBLANES)) & ((c2 % SUBLANES) <= (r2 % SUBLANES)) \
            & ((c2 % SUBLANES) < dec_seq)
        slope2 = jnp.exp2((r2 // SUBLANES + 1).astype(F32) * (-8.0 / ATT_HEADS))
        s = _dot_nt(q_ref[0], kn_ref[0]) * scale - slope2 * (r2 % SUBLANES - c2 % SUBLANES).astype(F32)
        s = jnp.where(ok, s, NEG)
        m_loc = jnp.max(s, axis=-1, keepdims=True)
        p = jnp.exp(s - m_loc)
        l_loc = jnp.sum(p, axis=-1, keepdims=True)
        a_loc = _dot(p.astype(BF16), vn_ref[0])

        m_tot = m_loc
        for jb in range(n_blk):
            m_tot = jnp.maximum(m_tot, jnp.where(selq[:, jb:jb + 1] > 0.5, m_scr[jb][:, 0:1], NEG))
        w_loc = jnp.exp(m_loc - m_tot)
        den = w_loc * l_loc
        num_w = jnp.zeros((hq, ATT_WIDTH), F32)
        for jb in range(n_blk):
            w = jnp.where(selq[:, jb:jb + 1] > 0.5, jnp.exp(m_scr[jb][:, 0:1] - m_tot), 0.0)
            num_w = num_w + w * acc_scr[jb]
            den = den + w * l_scr[jb][:, 0:1]
        num = jnp.concatenate(
            [num_w[h * SUBLANES:(h + 1) * SUBLANES, h * ATT_HEAD_DIM:(h + 1) * ATT_HEAD_DIM]
             for h in range(ATT_HEADS)], axis=0)
        o_ref[0] = (num + w_loc * a_loc) / den


def _samp_attn(q_s, k_s, v_s, cache_kt, cache_vt, page_table, dec_seq):
    db, n_pages = page_table.shape
    ppb = MOBA_BLOCK // PAGE_SIZE
    assert ppb == 2 and n_pages % ppb == 0 and dec_seq <= SUBLANES
    n_blk = n_pages // ppb
    assert 1 <= n_blk <= LANES
    bps = 2 if n_blk % 2 == 0 else 1
    past_len = n_pages * PAGE_SIZE
    hq = ATT_HEADS * SUBLANES

    def rows(t):
        t = t.reshape(db, dec_seq, ATT_HEADS, ATT_HEAD_DIM).transpose(0, 2, 1, 3)
        return jnp.pad(t, ((0, 0), (0, 0), (0, SUBLANES - dec_seq), (0, 0)))

    flat = lambda t: t.reshape(db, hq, ATT_HEAD_DIM).astype(BF16)
    q_r = rows(q_s)
    eye = jnp.eye(ATT_HEADS, dtype=q_r.dtype)
    q_bd = (q_r[:, :, :, None, :] * eye[None, :, None, :, None]).reshape(db, hq, ATT_WIDTH).astype(BF16)

    small = pl.BlockSpec((1, hq, ATT_HEAD_DIM), lambda b, s, pt: (b, 0, 0))

    def page(k):
        return pl.BlockSpec((None, ATT_WIDTH, PAGE_SIZE),
                            lambda b, s, pt: (pt[b * n_pages + ppb * bps * s + k], 0, 0))

    pages = [page(k) for k in range(ppb * bps)]
    out = pl.pallas_call(
        functools.partial(_samp_attn_body, n_blk, bps, past_len, dec_seq),
        grid_spec=pltpu.PrefetchScalarGridSpec(
            num_scalar_prefetch=1,
            grid=(db, n_blk // bps),
            in_specs=[pl.BlockSpec((1, hq, ATT_WIDTH), lambda b, s, pt: (b, 0, 0)), small, small, small]
            + pages + pages,
            out_specs=small,
            scratch_shapes=[pltpu.VMEM((n_blk, hq, LANES), F32), pltpu.VMEM((n_blk, hq, LANES), F32),
                            pltpu.VMEM((n_blk, hq, ATT_WIDTH), F32), pltpu.VMEM((hq, LANES), F32)]),
        out_shape=jax.ShapeDtypeStruct((db, hq, ATT_HEAD_DIM), F32),
        compiler_params=_params(("arbitrary", "arbitrary")),
        name="samp_attn",
    )(page_table.reshape(-1), q_bd, flat(q_r), flat(rows(k_s)), flat(rows(v_s)),
      *([cache_kt] * (ppb * bps)), *([cache_vt] * (ppb * bps)))
    out = out.reshape(db, ATT_HEADS, SUBLANES, ATT_HEAD_DIM)[:, :, :dec_seq]
    return out.transpose(0, 2, 1, 3).reshape(db * dec_seq, ATT_WIDTH)


def _mlstm_body(l_valid, qk_ref, v_ref, o_ref, g_ref, conv0_ref, c0_ref, n0_ref, m0_ref,
                cw_ref, cb_ref, gb_ref, ln_ref,
                hn_ref, convn_ref, c_ref, n_ref, m_ref, ubuf):
    c = pl.program_id(1)
    lc = LSTM_CHUNK
    hd = LSTM_HEAD_DIM

    @pl.when(c == 0)
    def _():
        ubuf[0:SUBLANES, :] = conv0_ref[...]
        c_ref[...] = c0_ref[...]
        n_ref[...] = n0_ref[...]
        m_ref[...] = m0_ref[...]

    u = qk_ref[...]
    ubuf[SUBLANES:SUBLANES + lc, :] = u
    y = cb_ref[...] + u * cw_ref[CONV_WIDTH - 1:CONV_WIDTH, :]
    for s in range(1, CONV_WIDTH):
        y = y + ubuf[SUBLANES - s:SUBLANES - s + lc, :] * cw_ref[CONV_WIDTH - 1 - s:CONV_WIDTH - s, :]
    qk = y * jax.nn.sigmoid(y)
    convn_ref[0] = ubuf[SUBLANES + l_valid - (CONV_WIDTH - 1):SUBLANES + l_valid, :]
    ubuf[0:SUBLANES, :] = ubuf[lc:lc + SUBLANES, :]

    row = lax.broadcasted_iota(jnp.int32, (lc, lc), 0)
    col = lax.broadcasted_iota(jnp.int32, (lc, lc), 1)
    tri = col <= row
    g = g_ref[...] + gb_ref[...]
    log_f = jnp.minimum(g, 0.0) - jnp.log1p(jnp.exp(-jnp.abs(g)))
    is_f = (col >= LSTM_HEADS) & (col < 2 * LSTM_HEADS)
    gates = jnp.where(is_f, log_f, g)
    if l_valid < lc:
        gates = jnp.where(row < l_valid, gates, jnp.where(is_f, 0.0, NEG))
    csum = jnp.dot(jnp.where(tri, 1.0, 0.0), jnp.where(is_f, gates, 0.0),
                   precision=_HI, preferred_element_type=F32)
    gates_t = gates.T
    csum_t = csum.T

    for h in range(LSTM_HEADS):
        hs = slice(h * hd, (h + 1) * hd)
        qf = qk[:, hs]
        kf = qk[:, LSTM_WIDTH + h * hd:LSTM_WIDTH + (h + 1) * hd] * (hd ** -0.5)
        vf = v_ref[:, hs]
        qb, kb, vb = qf.astype(BF16), kf.astype(BF16), vf.astype(BF16)
        b_col = csum[:, LSTM_HEADS + h:LSTM_HEADS + h + 1]
        b_row = csum_t[LSTM_HEADS + h:LSTM_HEADS + h + 1, :]
        ig_row = gates_t[h:h + 1, :]
        ig_col = gates[:, h:h + 1]
        m0 = m_ref[0, h:h + 1, 0:1]
        n0 = n_ref[0, h:h + 1, :]
        c0 = c_ref[0, h]

        log_d = jnp.where(tri, b_col - b_row + ig_row, NEG)
        dec0 = b_col + m0
        m_t = jnp.maximum(dec0, jnp.max(log_d, axis=-1, keepdims=True))
        d_m = jnp.exp(log_d - m_t)
        w0 = jnp.exp(dec0 - m_t)
        s = _dot_nt(qb, kb) * d_m
        num = w0 * _dot_nt(qb, c0.astype(BF16)) + _dot(s.astype(BF16), vb)
        den = w0 * jnp.sum(qf * n0, axis=-1, keepdims=True) + jnp.sum(s, axis=-1, keepdims=True)
        hh = num / jnp.maximum(jnp.abs(den), jnp.exp(-m_t))

        m_l = m_t[lc - 1:lc, :]
        b_l = b_col[lc - 1:lc, :]
        w_l = jnp.exp(b_l - b_col + ig_col - m_l)
        w0_l = jnp.exp(b_l + m0 - m_l)
        c_ref[0, h] = w0_l * c0 + _dot((w_l * vf).T.astype(BF16), kb)
        n_ref[0, h:h + 1, :] = w0_l * n0 + jnp.sum(w_l * kf, axis=0, keepdims=True)
        m_ref[0, h:h + 1, :] = jnp.broadcast_to(m_l, (1, LANES))

        hn = hh * lax.rsqrt(jnp.mean(hh * hh, axis=-1, keepdims=True) + NORM_EPS)
        hn = hn * ln_ref[:, hs] * jax.nn.sigmoid(o_ref[:, hs])
        hn_ref[:, hs] = hn.astype(hn_ref.dtype)


def _mlstm(qk, v, o, gates, conv0, c0, n0, m0, conv_w, conv_b, gate_b, lstm_norm, batch, n_chunks, l_valid):
    assert CONV_WIDTH - 1 <= l_valid <= LSTM_CHUNK
    nc = n_chunks
    tok = lambda b, c: (b * nc + c, 0)
    per_b3 = lambda b, c: (b, 0, 0)
    const = lambda b, c: (0, 0)
    rows = batch * nc * LSTM_CHUNK
    return pl.pallas_call(
        functools.partial(_mlstm_body, l_valid),
        grid=(batch, nc),
        in_specs=[pl.BlockSpec((LSTM_CHUNK, 2 * LSTM_WIDTH), tok),
                  pl.BlockSpec((LSTM_CHUNK, LSTM_WIDTH), tok),
                  pl.BlockSpec((LSTM_CHUNK, LSTM_WIDTH), tok),
                  pl.BlockSpec((LSTM_CHUNK, LANES), tok),
                  pl.BlockSpec((SUBLANES, 2 * LSTM_WIDTH), lambda b, c: (b, 0)),
                  pl.BlockSpec((1, LSTM_HEADS, LSTM_HEAD_DIM, LSTM_HEAD_DIM), lambda b, c: (b, 0, 0, 0)),
                  pl.BlockSpec((1, SUBLANES, LANES), per_b3),
                  pl.BlockSpec((1, SUBLANES, LANES), per_b3),
                  pl.BlockSpec((CONV_WIDTH, 2 * LSTM_WIDTH), const),
                  pl.BlockSpec((1, 2 * LSTM_WIDTH), const),
                  pl.BlockSpec((1, LANES), const),
                  pl.BlockSpec((1, LSTM_WIDTH), const)],
        out_specs=[pl.BlockSpec((LSTM_CHUNK, LSTM_WIDTH), tok),
                   pl.BlockSpec((1, CONV_WIDTH - 1, 2 * LSTM_WIDTH), per_b3),
                   pl.BlockSpec((1, LSTM_HEADS, LSTM_HEAD_DIM, LSTM_HEAD_DIM), lambda b, c: (b, 0, 0, 0)),
                   pl.BlockSpec((1, SUBLANES, LANES), per_b3),
                   pl.BlockSpec((1, SUBLANES, LANES), per_b3)],
        out_shape=[jax.ShapeDtypeStruct((rows, LSTM_WIDTH), BF16),
                   jax.ShapeDtypeStruct((batch, CONV_WIDTH - 1, 2 * LSTM_WIDTH), F32),
                   jax.ShapeDtypeStruct((batch, LSTM_HEADS, LSTM_HEAD_DIM, LSTM_HEAD_DIM), F32),
                   jax.ShapeDtypeStruct((batch, SUBLANES, LANES), F32),
                   jax.ShapeDtypeStruct((batch, SUBLANES, LANES), F32)],
        scratch_shapes=[pltpu.VMEM((LSTM_CHUNK + 2 * SUBLANES, 2 * LSTM_WIDTH), F32)],
        compiler_params=_params(("arbitrary", "arbitrary")),
        name="mlstm",
    )(qk, v, o, gates, conv0, c0, n0, m0, conv_w, conv_b, gate_b, lstm_norm)


def _back_body(n_p, ap_ref, as_ref, hp_ref, hs_ref, xp_ref, xs_ref, woa_ref, wob_ref, g_ref, wr_ref, br_ref,
               x1_ref, h2_ref, rt_ref):
    i = pl.program_id(0)
    is_p = i < n_p
    att = jnp.where(is_p, ap_ref[...], as_ref[...])
    hn = jnp.where(is_p, hp_ref[...], hs_ref[...])
    x = jnp.where(is_p, xp_ref[...], xs_ref[...])
    x1 = x + _dot(att, woa_ref[...]) + _dot(hn, wob_ref[...])
    x1_ref[...] = x1
    ms = jnp.mean(x1 * x1, axis=-1, keepdims=True)
    h2 = (x1 * lax.rsqrt(ms + NORM_EPS) * g_ref[...]).astype(BF16)
    h2_ref[...] = h2
    logits = _dot(h2, wr_ref[...]) + br_ref[...]
    lane = lax.broadcasted_iota(jnp.int32, logits.shape, 1).astype(F32)
    cur = logits
    vals, ids = [], []
    for _ in range(TOP_K):
        mk = jnp.max(cur, axis=-1, keepdims=True)
        ik = jnp.min(jnp.where(cur == mk, lane, float(LANES)), axis=-1, keepdims=True)
        cur = jnp.where(lane == ik, -3e38, cur)
        vals.append(mk)
        ids.append(ik)
    es = [jnp.exp(v - vals[0]) for v in vals]
    tot = es[0]
    for e in es[1:]:
        tot = tot + e
    out = jnp.zeros(logits.shape, F32)
    for k in range(TOP_K):
        out = jnp.where(lane == float(k), ids[k], out)
        out = jnp.where(lane == float(TOP_K + k), es[k] / tot, out)
    rt_ref[...] = out


def _back(att_p, att_s, hn_p, hn_s, xp, xs, wo_a, wo_b, g, w_r, b_r):
    t_p, t_s = xp.shape[0], xs.shape[0]
    tm = t_s
    n_p = t_p // tm
    t_all = t_p + t_s
    row = lambda i: (i, 0)
    prow = lambda i: (jnp.minimum(i, n_p - 1), 0)
    const = lambda i: (0, 0)
    return pl.pallas_call(
        functools.partial(_back_body, n_p),
        grid=(n_p + 1,),
        in_specs=[pl.BlockSpec((tm, ATT_WIDTH), prow), pl.BlockSpec((tm, ATT_WIDTH), const),
                  pl.BlockSpec((tm, LSTM_WIDTH), prow), pl.BlockSpec((tm, LSTM_WIDTH), const),
                  pl.BlockSpec((tm, D_MODEL), prow), pl.BlockSpec((tm, D_MODEL), const),
                  pl.BlockSpec((ATT_WIDTH, D_MODEL), const), pl.BlockSpec((LSTM_WIDTH, D_MODEL), const),
                  pl.BlockSpec((1, D_MODEL), const), pl.BlockSpec((D_MODEL, LANES), const),
                  pl.BlockSpec((1, LANES), const)],
        out_specs=[pl.BlockSpec((tm, D_MODEL), row), pl.BlockSpec((tm, D_MODEL), row),
                   pl.BlockSpec((tm, LANES), row)],
        out_shape=[jax.ShapeDtypeStruct((t_all, D_MODEL), F32), jax.ShapeDtypeStruct((t_all, D_MODEL), BF16),
                   jax.ShapeDtypeStruct((t_all, LANES), F32)],
        compiler_params=_params(("arbitrary",)),
        name="back",
    )(att_p, att_s, hn_p, hn_s, xp, xs, wo_a, wo_b, g, w_r, b_r)


def _expert_body(be_ref, nu_ref, x_ref, wu_ref, bu_ref, wd_ref, bd_ref, y_ref):
    del be_ref

    @pl.when(pl.program_id(0) < nu_ref[0])
    def _():
        xb = x_ref[...]
        acc = None
        for c in range(D_FF // COL_CHUNK):
            lo, hi = c * COL_CHUNK, (c + 1) * COL_CHUNK
            glu = _dot(xb, wu_ref[0, :, lo:hi]) + bu_ref[0, :, lo:hi]
            lin = _dot(xb, wu_ref[0, :, D_FF + lo:D_FF + hi]) + bu_ref[0, :, D_FF + lo:D_FF + hi]
            glu = jnp.minimum(glu, SWIGLU_LIMIT)
            lin = jnp.clip(lin, -SWIGLU_LIMIT, SWIGLU_LIMIT)
            mid = (glu * jax.nn.sigmoid(SWIGLU_ALPHA * glu) * (lin + 1.0)).astype(BF16)
            part = _dot(mid, wd_ref[0, lo:hi, :])
            acc = part if acc is None else acc + part
        y_ref[...] = acc + bd_ref[0]


def _experts(xs, blk_expert, n_used, w_up, b_up, w_down, b_down):
    cap = xs.shape[0]
    n_blocks = cap // EXPERT_ROWS
    live = lambda i, be, nu: jnp.minimum(i, nu[0] - 1)
    return pl.pallas_call(
        _expert_body,
        grid_spec=pltpu.PrefetchScalarGridSpec(
            num_scalar_prefetch=2,
            grid=(n_blocks,),
            in_specs=[pl.BlockSpec((EXPERT_ROWS, D_MODEL), lambda i, be, nu: (live(i, be, nu), 0)),
                      pl.BlockSpec((1, D_MODEL, 2 * D_FF), lambda i, be, nu: (be[live(i, be, nu)], 0, 0)),
                      pl.BlockSpec((1, 1, 2 * D_FF), lambda i, be, nu: (be[live(i, be, nu)], 0, 0)),
                      pl.BlockSpec((1, D_FF, D_MODEL), lambda i, be, nu: (be[live(i, be, nu)], 0, 0)),
                      pl.BlockSpec((1, 1, D_MODEL), lambda i, be, nu: (be[live(i, be, nu)], 0, 0))],
            out_specs=pl.BlockSpec((EXPERT_ROWS, D_MODEL), lambda i, be, nu: (live(i, be, nu), 0))),
        out_shape=jax.ShapeDtypeStruct((cap, D_MODEL), F32),
        compiler_params=_params(("arbitrary",)),
        name="experts",
    )(blk_expert, n_used, xs, w_up, b_up, w_down, b_down)


def _combine_body(n_p, x1_ref, yg_ref, rt_ref, g_ref, yp_ref, ys_ref):
    i = pl.program_id(0)
    y = x1_ref[...]
    rt = rt_ref[...]
    for k in range(TOP_K):
        y = y + yg_ref[k] * rt[:, TOP_K + k:TOP_K + k + 1]
    ms = jnp.mean(y * y, axis=-1, keepdims=True)
    out = y * lax.rsqrt(ms + NORM_EPS) * g_ref[...]

    @pl.when(i < n_p)
    def _():
        yp_ref[...] = out

    @pl.when(i >= n_p)
    def _():
        ys_ref[...] = out


def _combine(x1, yg, rt, g, t_p, t_s):
    tm = min(256, t_s)
    assert t_p % tm == 0 and t_s % tm == 0
    n_p, n_s = t_p // tm, t_s // tm
    row = lambda i: (i, 0)
    return pl.pallas_call(
        functools.partial(_combine_body, n_p),
        grid=(n_p + n_s,),
        in_specs=[pl.BlockSpec((tm, D_MODEL), row), pl.BlockSpec((TOP_K, tm, D_MODEL), lambda i: (0, i, 0)),
                  pl.BlockSpec((tm, LANES), row), pl.BlockSpec((1, D_MODEL), lambda i: (0, 0))],
        out_specs=[pl.BlockSpec((tm, D_MODEL), lambda i: (jnp.minimum(i, n_p - 1), 0)),
                   pl.BlockSpec((tm, D_MODEL), lambda i: (jnp.maximum(i - n_p, 0), 0))],
        out_shape=[jax.ShapeDtypeStruct((t_p, D_MODEL), F32), jax.ShapeDtypeStruct((t_s, D_MODEL), F32)],
        compiler_params=_params(("arbitrary",)),
        name="combine",
    )(x1, yg, rt, g)


def _route(top_idx):
    t = top_idx.shape[0]
    n = t * TOP_K
    bm = EXPERT_ROWS
    e_flat = top_idx.reshape(-1)
    e_sorted, order = lax.sort((e_flat, jnp.arange(n, dtype=jnp.int32)), num_keys=1, is_stable=True)
    experts = jnp.arange(N_EXPERTS, dtype=jnp.int32)
    counts = jnp.sum((e_flat[:, None] == experts[None, :]).astype(jnp.int32), axis=0)
    padded = ((counts + bm - 1) // bm) * bm
    pad_end = jnp.cumsum(padded)
    pad_start = pad_end - padded
    grp_start = jnp.cumsum(counts) - counts
    n_blocks = -(-n // bm) + N_EXPERTS
    blk_start = jnp.arange(n_blocks, dtype=jnp.int32) * bm
    blk_expert = jnp.minimum(jnp.sum((pad_end[None, :] <= blk_start[:, None]).astype(jnp.int32), axis=1),
                             N_EXPERTS - 1)
    n_used = (pad_end[-1] // bm).astype(jnp.int32).reshape(1)
    shift = (pad_start - grp_start).astype(jnp.int32)
    row_shift = jnp.broadcast_to(shift[blk_expert][:, None], (n_blocks, bm)).reshape(-1)
    row_pos = jnp.clip(jnp.arange(n_blocks * bm, dtype=jnp.int32) - row_shift, 0, n - 1)
    row_tok = order[row_pos] // TOP_K
    shift_sorted = jnp.sum(jnp.where(e_sorted[:, None] == experts[None, :], shift[None, :], 0), axis=1)
    dest_sorted = jnp.arange(n, dtype=jnp.int32) + shift_sorted
    _, dest = lax.sort((order, dest_sorted), num_keys=1)
    return row_tok, blk_expert, n_used, dest.reshape(t, TOP_K).T.reshape(-1)


def kernel(x_prompt, x_sample, cache_k, cache_v, page_table, state_conv, state_C, state_n, state_m,
           norm_mix, w_in, conv_w, conv_b, gate_b, lstm_norm, w_out, norm_ffn, w_router, b_router,
           w_up, b_up, w_down, b_down, norm_final):
    assert w_in.shape[0] == 1, "single-layer stack"
    batch, seq, _ = x_prompt.shape
    db, ds, _ = x_sample.shape
    t_p, t_s = batch * seq, db * ds
    n_pool = cache_k.shape[1]
    assert seq % LSTM_CHUNK == 0 and page_table.shape[1] * PAGE_SIZE % MOBA_BLOCK == 0

    xp = x_prompt.reshape(t_p, D_MODEL)
    xs = x_sample.reshape(t_s, D_MODEL)
    w_main = w_in[0, :, :MAIN_COLS].astype(BF16)
    w_gate = jnp.pad(w_in[0, :, MAIN_COLS:], ((0, 0), (0, LANES - 2 * LSTM_HEADS))).astype(BF16)
    q_a, k_a, v_a, qk_src, v_l, o_l, gts = _front(xp, xs, norm_mix, w_main, w_gate)

    att_p = _moba(q_a, k_a, v_a, batch, seq)
    k_s, v_s = k_a[t_p:], v_a[t_p:]

    def key_minor(cache):
        cache = cache.reshape(n_pool, ATT_HEADS, PAGE_SIZE, ATT_HEAD_DIM).transpose(0, 1, 3, 2)
        return cache.reshape(n_pool, ATT_WIDTH, PAGE_SIZE)

    att_s = _samp_attn(q_a[t_p:], k_s, v_s, key_minor(cache_k), key_minor(cache_v), page_table, ds).astype(BF16)

    gate_b_row = jnp.pad(gate_b, ((0, 0), (0, LANES - 2 * LSTM_HEADS)))
    cw, cb = conv_w[0], conv_b
    hn_p, conv_p, c_p, n_p_, m_p = _mlstm(
        qk_src, v_l, o_l, gts,
        jnp.zeros((batch * SUBLANES, 2 * LSTM_WIDTH), F32),
        jnp.zeros((batch, LSTM_HEADS, LSTM_HEAD_DIM, LSTM_HEAD_DIM), F32),
        jnp.zeros((batch, SUBLANES, LANES), F32), jnp.zeros((batch, SUBLANES, LANES), F32),
        cw, cb, gate_b_row, lstm_norm, batch, seq // LSTM_CHUNK, LSTM_CHUNK)

    def pad_chunk(a):
        a = a[t_p:].reshape(db, ds, a.shape[-1])
        return jnp.pad(a, ((0, 0), (0, LSTM_CHUNK - ds), (0, 0))).reshape(db * LSTM_CHUNK, a.shape[-1])

    conv0_s = jnp.pad(state_conv[0], ((0, 0), (SUBLANES - (CONV_WIDTH - 1), 0), (0, 0)))
    n0_s = jnp.pad(state_n[0], ((0, 0), (0, SUBLANES - LSTM_HEADS), (0, 0)))
    m0_s = jnp.broadcast_to(jnp.pad(state_m[0], ((0, 0), (0, SUBLANES - LSTM_HEADS)))[:, :, None],
                            (db, SUBLANES, LANES))
    hn_s, conv_s, c_s, n_s, m_s = _mlstm(
        pad_chunk(qk_src), pad_chunk(v_l), pad_chunk(o_l), pad_chunk(gts),
        conv0_s.reshape(db * SUBLANES, 2 * LSTM_WIDTH), state_C[0], n0_s, m0_s,
        cw, cb, gate_b_row, lstm_norm, db, 1, ds)
    hn_s = hn_s.reshape(db, LSTM_CHUNK, LSTM_WIDTH)[:, :ds].reshape(t_s, LSTM_WIDTH)

    wo = w_out[0].astype(BF16)
    w_r = jnp.pad(w_router[0], ((0, 0), (0, LANES - N_EXPERTS))).astype(BF16)
    b_r = jnp.pad(b_router, ((0, 0), (0, LANES - N_EXPERTS)), constant_values=NEG)
    x1, h2, rt = _back(att_p, att_s, hn_p, hn_s, xp, xs, wo[:ATT_WIDTH], wo[ATT_WIDTH:], norm_ffn, w_r, b_r)

    row_tok, blk_expert, n_used, dest = _route(rt[:, :TOP_K].astype(jnp.int32))
    ys = _experts(h2[row_tok], blk_expert, n_used, w_up[0].astype(BF16), b_up[0][:, None, :],
                  w_down[0].astype(BF16), b_down[0][:, None, :])
    yg = ys[dest].reshape(TOP_K, t_p + t_s, D_MODEL)
    y_p, y_s = _combine(x1, yg, rt, norm_final.reshape(1, D_MODEL), t_p, t_s)

    def to_pages(t):
        t = t[:t_p].reshape(batch, seq // PAGE_SIZE, PAGE_SIZE, ATT_HEADS, ATT_HEAD_DIM)
        return t.transpose(0, 1, 3, 2, 4)[None]

    def to_slots(t):
        return t.reshape(db, ds, ATT_HEADS, ATT_HEAD_DIM).transpose(0, 2, 1, 3)[None]

    return (y_p.reshape(batch, seq, D_MODEL), y_s.reshape(db, ds, D_MODEL),
            to_pages(k_a), to_pages(v_a), to_slots(k_s), to_slots(v_s),
            conv_p[None], conv_s[None],
            c_p[None], n_p_[None, :, :LSTM_HEADS, :], m_p[None, :, :LSTM_HEADS, 0],
            c_s[None], n_s[None, :, :LSTM_HEADS, :], m_s[None, :, :LSTM_HEADS, 0])
```

```python
import functools

import jax
import jax.numpy as jnp
from jax import lax
from jax.experimental import pallas as pl
from jax.experimental.pallas import tpu as pltpu

F32 = jnp.float32
BF16 = jnp.bfloat16

D_MODEL = 1024
ATT_HEADS = 8
ATT_HEAD_DIM = 64
ATT_WIDTH = ATT_HEADS * ATT_HEAD_DIM
MOBA_BLOCK = 256
MOBA_TOPK = 3
LSTM_HEADS = 4
LSTM_HEAD_DIM = 128
LSTM_WIDTH = LSTM_HEADS * LSTM_HEAD_DIM
CONV_WIDTH = 4
LSTM_CHUNK = 128
PAGE_SIZE = 128
N_EXPERTS = 32
TOP_K = 4
D_FF = D_MODEL
SWIGLU_LIMIT = 7.0
SWIGLU_ALPHA = 1.702
NORM_EPS = 1e-5
MAIN_COLS = 3 * ATT_WIDTH + 4 * LSTM_WIDTH
COL_CHUNK = 512
LANES = 128
SUBLANES = 8
EXPERT_ROWS = 512
NEG = -1e30
LOG2E = 1.4426950408889634
N_FEAT = 4
SCORE_AHEAD = 4
VMEM_LIMIT = 48 * 1024 * 1024
EXPERT_VMEM_LIMIT = 56 * 1024 * 1024

_NT = (((1,), (1,)), ((), ()))
_HI = lax.Precision.HIGHEST


def _dot(a, b):
    return jnp.dot(a, b, preferred_element_type=F32)


def _dot_nt(a, b, precision=None):
    return lax.dot_general(a, b, _NT, precision=precision, preferred_element_type=F32)


def _params(sem, vmem_limit=VMEM_LIMIT):
    return pltpu.CompilerParams(dimension_semantics=sem, vmem_limit_bytes=vmem_limit)


def _front_body(n_p, xp_ref, xs_ref, g_ref, w_ref, wg_ref,
                qa_ref, ka_ref, va_ref, qk_ref, vl_ref, ol_ref, gt_ref):
    i = pl.program_id(0)
    x = jnp.where(i < n_p, xp_ref[...], xs_ref[...])
    ms = jnp.mean(x * x, axis=-1, keepdims=True)
    h = (x * lax.rsqrt(ms + NORM_EPS) * g_ref[...]).astype(BF16)
    dests = ((qa_ref, 0), (ka_ref, 0), (va_ref, 0), (qk_ref, 0), (qk_ref, 1), (vl_ref, 0), (ol_ref, 0))
    for c, (ref, part) in enumerate(dests):
        z = _dot(h, w_ref[:, c * COL_CHUNK:(c + 1) * COL_CHUNK])
        ref[:, part * COL_CHUNK:(part + 1) * COL_CHUNK] = z.astype(ref.dtype)
    gt_ref[...] = _dot(h, wg_ref[...])


def _front(xp, xs, g, w_main, w_gate):
    t_p, t_s = xp.shape[0], xs.shape[0]
    tm = t_s
    assert t_p % tm == 0 and tm % 128 == 0
    n_p = t_p // tm
    t_all = t_p + t_s
    row = lambda i: (i, 0)
    const = lambda i: (0, 0)
    wide = lambda n, dt: jax.ShapeDtypeStruct((t_all, n), dt)
    return pl.pallas_call(
        functools.partial(_front_body, n_p),
        grid=(n_p + 1,),
        in_specs=[pl.BlockSpec((tm, D_MODEL), lambda i: (jnp.minimum(i, n_p - 1), 0)),
                  pl.BlockSpec((tm, D_MODEL), const),
                  pl.BlockSpec((1, D_MODEL), const),
                  pl.BlockSpec((D_MODEL, MAIN_COLS), const),
                  pl.BlockSpec((D_MODEL, LANES), const)],
        out_specs=[pl.BlockSpec((tm, ATT_WIDTH), row), pl.BlockSpec((tm, ATT_WIDTH), row),
                   pl.BlockSpec((tm, ATT_WIDTH), row), pl.BlockSpec((tm, 2 * LSTM_WIDTH), row),
                   pl.BlockSpec((tm, LSTM_WIDTH), row), pl.BlockSpec((tm, LSTM_WIDTH), row),
                   pl.BlockSpec((tm, LANES), row)],
        out_shape=[wide(ATT_WIDTH, BF16), wide(ATT_WIDTH, F32), wide(ATT_WIDTH, F32),
                   wide(2 * LSTM_WIDTH, F32), wide(LSTM_WIDTH, F32), wide(LSTM_WIDTH, F32),
                   wide(LANES, F32)],
        compiler_params=_params(("arbitrary",)),
        name="front",
    )(xp, xs, g, w_main, w_gate)


def _topk_rows(g, n_valid, k_sel):
    n = g.shape[0]
    jj = lax.broadcasted_iota(jnp.int32, g.shape, 0)
    rank = jnp.zeros(g.shape, F32)
    for j2 in range(n):
        row = g[j2:j2 + 1, :]
        beats = jnp.where(row > g, 1.0, jnp.where((row == g) & (jj > j2), 1.0, 0.0))
        rank = rank + jnp.where(j2 < n_valid, beats, 0.0)
    return jnp.where((jj < n_valid) & (rank < k_sel), 1.0, 0.0)


def _topk_lanes(g, n, k_sel):
    jj = lax.broadcasted_iota(jnp.int32, g.shape, 1)
    rank = jnp.zeros(g.shape, F32)
    for j2 in range(n):
        col = g[:, j2:j2 + 1]
        rank = rank + jnp.where(col > g, 1.0, jnp.where((col == g) & (jj > j2), 1.0, 0.0))
    return jnp.where((jj < n) & (rank < k_sel), 1.0, 0.0)


def _moba_body(nb, q_ref, k_ref, v_ref, o_ref, kaug, vt, kmean):
    i = pl.program_id(1)
    blk = MOBA_BLOCK
    hd = ATT_HEAD_DIM
    seq = nb * blk
    k_sel = min(MOBA_TOPK, nb - 1)
    heads_per = LANES // hd

    @pl.when(i == 0)
    def _():
        row = lax.broadcasted_iota(jnp.int32, (seq, LANES), 0)
        lane_s = lax.broadcasted_iota(jnp.int32, (seq, LANES), 1)
        for c in range(ATT_WIDTH // LANES):
            kf = k_ref[:, c * LANES:(c + 1) * LANES]
            kmean[c] = jnp.sum(kf.reshape(nb, blk, LANES), axis=1) * (1.0 / blk)
            for hl in range(heads_per):
                f = lane_s - (heads_per - 1 - hl) * hd
                feat = jnp.where(f == 0, (row % blk).astype(F32),
                                 jnp.where(f == 1, (row // blk * blk).astype(F32),
                                           jnp.where((f == 2) | (f == 3), 1.0,
                                                     jnp.where(f - N_FEAT == row // blk, 1.0, 0.0))))
                kaug[c * heads_per + hl] = jnp.where((lane_s // hd) == hl, kf, feat).astype(BF16)
        for jb in range(nb):
            vt[jb] = v_ref[jb * blk:(jb + 1) * blk, :].T.astype(BF16)

    key_io = lax.broadcasted_iota(jnp.int32, (blk, blk), 0)
    qry_io = lax.broadcasted_iota(jnp.int32, (blk, blk), 1)
    causal = key_io <= qry_io
    q_row = lax.broadcasted_iota(jnp.int32, (blk, LANES), 0)
    lane = lax.broadcasted_iota(jnp.int32, (blk, LANES), 1)
    blk_io = lax.broadcasted_iota(jnp.int32, (nb, blk), 0)
    own = pl.multiple_of(i * blk, blk)

    q_augs = []
    for hp in range(ATT_WIDTH // LANES):
        sl = slice(hp * LANES, (hp + 1) * LANES)
        q2 = q_ref[:, sl].astype(F32) * (hd ** -0.5)
        km = kmean[hp]
        for hl in range(heads_per):
            head = hp * heads_per + hl
            slope = float(2.0 ** (-8.0 * (head + 1) / ATT_HEADS))
            inhead = (lane // hd) == hl
            gate_t = _dot_nt(km, jnp.where(inhead, q2, 0.0), _HI)
            keep = (_topk_rows(gate_t, i, k_sel) > 0.5) | (blk_io == i)
            spare = (heads_per - 1 - hl) * hd + N_FEAT
            mask_t = jnp.concatenate([jnp.zeros((spare, blk), F32), jnp.where(keep, 0.0, NEG),
                                      jnp.zeros((LANES - spare - nb, blk), F32)], axis=0)
            f = lane - (heads_per - 1 - hl) * hd
            feat = jnp.where(f <= 1, slope,
                             jnp.where(f == 2, -slope * (i * blk).astype(F32),
                                       jnp.where(f == 3, -slope * q_row.astype(F32), mask_t.T)))
            q_augs.append(jnp.where(inhead, q2, feat).astype(BF16))

    def block_pass(j, off, carry):
        def scores(h):
            return _dot_nt(kaug[h, pl.ds(off, blk), :], q_augs[h]) * LOG2E

        ahead = {h: scores(h) for h in range(min(SCORE_AHEAD, ATT_HEADS))}
        out = ()
        for h in range(ATT_HEADS):
            t = ahead.pop(h)
            if carry is None:
                t = jnp.where(causal, t, NEG)
                m_new = jnp.max(t, axis=0, keepdims=True)
            else:
                m, l, acc = carry[3 * h:3 * h + 3]
                m_new = jnp.maximum(m, jnp.max(t, axis=0, keepdims=True))
            p = jnp.exp2(t - m_new)
            if h + SCORE_AHEAD < ATT_HEADS:
                ahead[h + SCORE_AHEAD] = scores(h + SCORE_AHEAD)
            pv = _dot(vt[j, h * hd:(h + 1) * hd, :], p.astype(BF16))
            if carry is None:
                out = out + (m_new, jnp.sum(p, axis=0, keepdims=True), pv)
            else:
                a = jnp.exp2(m - m_new)
                out = out + (m_new, a * l + jnp.sum(p, axis=0, keepdims=True), a * acc + pv)
        return out

    carry = block_pass(i, own, None)
    carry = lax.fori_loop(0, i, lambda j, c: block_pass(j, pl.multiple_of(j * blk, blk), c), carry)
    for hp in range(ATT_WIDTH // LANES):
        o_t = jnp.concatenate([carry[3 * h + 2] / carry[3 * h + 1]
                               for h in range(hp * heads_per, (hp + 1) * heads_per)], axis=0)
        o_ref[:, hp * LANES:(hp + 1) * LANES] = o_t.T.astype(o_ref.dtype)


def _moba(q_all, k_all, v_all, batch, seq):
    nb = seq // MOBA_BLOCK
    assert seq % MOBA_BLOCK == 0 and N_FEAT + nb <= ATT_HEAD_DIM and ATT_HEADS == 8
    return pl.pallas_call(
        functools.partial(_moba_body, nb),
        grid=(batch, nb),
        in_specs=[pl.BlockSpec((MOBA_BLOCK, ATT_WIDTH), lambda b, i: (b * nb + i, 0)),
                  pl.BlockSpec((seq, ATT_WIDTH), lambda b, i: (b, 0)),
                  pl.BlockSpec((seq, ATT_WIDTH), lambda b, i: (b, 0))],
        out_specs=pl.BlockSpec((MOBA_BLOCK, ATT_WIDTH), lambda b, i: (b * nb + i, 0)),
        out_shape=jax.ShapeDtypeStruct((batch * seq, ATT_WIDTH), BF16),
        scratch_shapes=[pltpu.VMEM((ATT_HEADS, seq, LANES), BF16), pltpu.VMEM((nb, ATT_WIDTH, MOBA_BLOCK), BF16),
                        pltpu.VMEM((ATT_WIDTH // LANES, nb, LANES), F32)],
        compiler_params=_params(("arbitrary", "arbitrary")),
        name="moba",
    )(q_all, k_all, v_all)


def _samp_attn_body(n_blk, bps, past_len, dec_seq, pt_ref, qbd_ref, q_ref, kn_ref, vn_ref, *rest):
    del pt_ref
    ppb = MOBA_BLOCK // PAGE_SIZE
    k_refs, v_refs = rest[:ppb * bps], rest[ppb * bps:2 * ppb * bps]
    o_ref, m_scr, l_scr, acc_scr, g_scr = rest[2 * ppb * bps:]
    step = pl.program_id(1)
    hq = ATT_HEADS * SUBLANES
    scale = ATT_HEAD_DIM ** -0.5
    qbd = qbd_ref[0]
    wide = bps * MOBA_BLOCK
    rows = lax.broadcasted_iota(jnp.int32, (hq, wide), 0)
    slope = jnp.exp2((rows // SUBLANES + 1).astype(F32) * (-8.0 / ATT_HEADS))
    tq = (past_len + rows % SUBLANES).astype(F32)
    pos = (step * wide + lax.broadcasted_iota(jnp.int32, (hq, wide), 1)).astype(F32)
    lanes = lax.broadcasted_iota(jnp.int32, (hq, LANES), 1)

    @pl.when(step == 0)
    def _():
        g_scr[...] = jnp.zeros(g_scr.shape, F32)
        m_scr[...] = jnp.zeros(m_scr.shape, F32)
        l_scr[...] = jnp.zeros(l_scr.shape, F32)

    k_cat = jnp.concatenate([r[...].astype(BF16) for r in k_refs], axis=1)
    s_raw = _dot(qbd, k_cat)
    s = s_raw * scale - slope * (tq - pos)
    ones = jnp.ones((MOBA_BLOCK, LANES), BF16)
    g_all, m_all, l_all = g_scr[...], m_scr[...], l_scr[...]
    for bl in range(bps):
        j = step * bps + bl
        seg = slice(bl * MOBA_BLOCK, (bl + 1) * MOBA_BLOCK)
        m = jnp.max(s[:, seg], axis=-1, keepdims=True)
        p = jnp.exp(s[:, seg] - m).astype(BF16)
        l_rep = _dot(p, ones)
        v_cat = jnp.concatenate([v_refs[ppb * bl + pg][...].astype(BF16) for pg in range(ppb)], axis=1)
        acc_scr[j] = _dot_nt(p, v_cat)
        gate = jnp.sum(s_raw[:, seg], axis=-1, keepdims=True) * (1.0 / MOBA_BLOCK)
        hit = lanes == j
        g_all = jnp.where(hit, gate, g_all)
        m_all = jnp.where(hit, m, m_all)
        l_all = jnp.where(hit, l_rep, l_all)
    g_scr[...] = g_all
    m_scr[...] = m_all
    l_scr[...] = l_all

    @pl.when(step == n_blk // bps - 1)
    def _():
        chosen = _topk_lanes(g_all, n_blk, min(MOBA_TOPK, n_blk)) > 0.5

        r2 = lax.broadcasted_iota(jnp.int32, (hq, hq), 0)
        c2 = lax.broadcasted_iota(jnp.int32, (hq, hq), 1)
        ok = ((c2 // SUBLANES) == (r2 // SUBLANES)) & ((c2 % SUBLANES) <= (r2 % SUBLANES)) \
            & ((c2 % SUBLANES) < dec_seq)
        slope2 = jnp.exp2((r2 // SUBLANES + 1).astype(F32) * (-8.0 / ATT_HEADS))
        s = _dot_nt(q_ref[0], kn_ref[0]) * scale - slope2 * (r2 % SUBLANES - c2 % SUBLANES).astype(F32)
        s = jnp.where(ok, s, NEG)
        m_loc = jnp.max(s, axis=-1, keepdims=True)
        p = jnp.exp(s - m_loc)
        l_loc = jnp.sum(p, axis=-1, keepdims=True)
        a_loc = _dot(p.astype(BF16), vn_ref[0])

        m_tot = jnp.maximum(m_loc, jnp.max(jnp.where(chosen, m_all, NEG), axis=-1, keepdims=True))
        w_blk = jnp.where(chosen, jnp.exp(m_all - m_tot), 0.0)
        w_loc = jnp.exp(m_loc - m_tot)
        den = w_loc * l_loc + jnp.sum(w_blk * l_all, axis=-1, keepdims=True)
        num_w = jnp.zeros((hq, ATT_WIDTH), F32)
        for jb in range(n_blk):
            num_w = num_w + w_blk[:, jb:jb + 1] * acc_scr[jb]
        num = jnp.concatenate(
            [num_w[h * SUBLANES:(h + 1) * SUBLANES, h * ATT_HEAD_DIM:(h + 1) * ATT_HEAD_DIM]
             for h in range(ATT_HEADS)], axis=0)
        o_ref[0] = (num + w_loc * a_loc) / den


def _samp_attn(q_s, k_s, v_s, cache_kt, cache_vt, page_table, dec_seq):
    db, n_pages = page_table.shape
    ppb = MOBA_BLOCK // PAGE_SIZE
    assert ppb == 2 and n_pages % ppb == 0 and dec_seq <= SUBLANES
    n_blk = n_pages // ppb
    assert 1 <= n_blk <= LANES
    bps = next(c for c in (4, 2, 1) if n_blk % c == 0)
    past_len = n_pages * PAGE_SIZE
    hq = ATT_HEADS * SUBLANES

    def rows(t):
        t = t.reshape(db, dec_seq, ATT_HEADS, ATT_HEAD_DIM).transpose(0, 2, 1, 3)
        return jnp.pad(t, ((0, 0), (0, 0), (0, SUBLANES - dec_seq), (0, 0)))

    flat = lambda t: t.reshape(db, hq, ATT_HEAD_DIM).astype(BF16)
    q_r = rows(q_s)
    eye = jnp.eye(ATT_HEADS, dtype=q_r.dtype)
    q_bd = (q_r[:, :, :, None, :] * eye[None, :, None, :, None]).reshape(db, hq, ATT_WIDTH).astype(BF16)

    small = pl.BlockSpec((1, hq, ATT_HEAD_DIM), lambda b, s, pt: (b, 0, 0))

    def page(k):
        return pl.BlockSpec((None, ATT_WIDTH, PAGE_SIZE),
                            lambda b, s, pt: (pt[b * n_pages + ppb * bps * s + k], 0, 0))

    pages = [page(k) for k in range(ppb * bps)]
    out = pl.pallas_call(
        functools.partial(_samp_attn_body, n_blk, bps, past_len, dec_seq),
        grid_spec=pltpu.PrefetchScalarGridSpec(
            num_scalar_prefetch=1,
            grid=(db, n_blk // bps),
            in_specs=[pl.BlockSpec((1, hq, ATT_WIDTH), lambda b, s, pt: (b, 0, 0)), small, small, small]
            + pages + pages,
            out_specs=small,
            scratch_shapes=[pltpu.VMEM((hq, LANES), F32), pltpu.VMEM((hq, LANES), F32),
                            pltpu.VMEM((n_blk, hq, ATT_WIDTH), F32), pltpu.VMEM((hq, LANES), F32)]),
        out_shape=jax.ShapeDtypeStruct((db, hq, ATT_HEAD_DIM), F32),
        compiler_params=_params(("arbitrary", "arbitrary")),
        name="samp_attn",
    )(page_table.reshape(-1), q_bd, flat(q_r), flat(rows(k_s)), flat(rows(v_s)),
      *([cache_kt] * (ppb * bps)), *([cache_vt] * (ppb * bps)))
    out = out.reshape(db, ATT_HEADS, SUBLANES, ATT_HEAD_DIM)[:, :, :dec_seq]
    return out.transpose(0, 2, 1, 3).reshape(db * dec_seq, ATT_WIDTH)


def _mlstm_body(l_valid, qk_ref, v_ref, o_ref, g_ref, conv0_ref, c0_ref, n0_ref, m0_ref,
                cw_ref, cb_ref, gb_ref, ln_ref,
                hn_ref, convn_ref, c_ref, n_ref, m_ref, ubuf):
    c = pl.program_id(1)
    lc = LSTM_CHUNK
    hd = LSTM_HEAD_DIM

    @pl.when(c == 0)
    def _():
        ubuf[0:SUBLANES, :] = conv0_ref[...]
        c_ref[...] = c0_ref[...]
        n_ref[...] = n0_ref[...]
        m_ref[...] = m0_ref[...]

    u = qk_ref[...]
    ubuf[SUBLANES:SUBLANES + lc, :] = u
    y = cb_ref[...] + u * cw_ref[CONV_WIDTH - 1:CONV_WIDTH, :]
    for s in range(1, CONV_WIDTH):
        y = y + ubuf[SUBLANES - s:SUBLANES - s + lc, :] * cw_ref[CONV_WIDTH - 1 - s:CONV_WIDTH - s, :]
    qk = y * jax.nn.sigmoid(y)
    convn_ref[0] = ubuf[SUBLANES + l_valid - (CONV_WIDTH - 1):SUBLANES + l_valid, :]
    ubuf[0:SUBLANES, :] = ubuf[lc:lc + SUBLANES, :]

    row = lax.broadcasted_iota(jnp.int32, (lc, lc), 0)
    col = lax.broadcasted_iota(jnp.int32, (lc, lc), 1)
    tri = col <= row
    g = g_ref[...] + gb_ref[...]
    log_f = jnp.minimum(g, 0.0) - jnp.log1p(jnp.exp(-jnp.abs(g)))
    is_f = (col >= LSTM_HEADS) & (col < 2 * LSTM_HEADS)
    gates = jnp.where(is_f, log_f, g)
    if l_valid < lc:
        gates = jnp.where(row < l_valid, gates, jnp.where(is_f, 0.0, NEG))
    csum = jnp.dot(jnp.where(tri, 1.0, 0.0), jnp.where(is_f, gates, 0.0),
                   precision=_HI, preferred_element_type=F32)
    gates_t = gates.T
    csum_t = csum.T

    for h in range(LSTM_HEADS):
        hs = slice(h * hd, (h + 1) * hd)
        qf = qk[:, hs]
        kf = qk[:, LSTM_WIDTH + h * hd:LSTM_WIDTH + (h + 1) * hd] * (hd ** -0.5)
        vf = v_ref[:, hs]
        qb, kb, vb = qf.astype(BF16), kf.astype(BF16), vf.astype(BF16)
        b_col = csum[:, LSTM_HEADS + h:LSTM_HEADS + h + 1]
        b_row = csum_t[LSTM_HEADS + h:LSTM_HEADS + h + 1, :]
        ig_row = gates_t[h:h + 1, :]
        ig_col = gates[:, h:h + 1]
        m0 = m_ref[0, h:h + 1, 0:1]
        n0 = n_ref[0, h:h + 1, :]
        c0 = c_ref[0, h]

        log_d = jnp.where(tri, b_col - b_row + ig_row, NEG)
        dec0 = b_col + m0
        m_t = jnp.maximum(dec0, jnp.max(log_d, axis=-1, keepdims=True))
        d_m = jnp.exp(log_d - m_t)
        w0 = jnp.exp(dec0 - m_t)
        s = _dot_nt(qb, kb) * d_m
        num = w0 * _dot_nt(qb, c0.astype(BF16)) + _dot(s.astype(BF16), vb)
        den = w0 * jnp.sum(qf * n0, axis=-1, keepdims=True) + jnp.sum(s, axis=-1, keepdims=True)
        hh = num / jnp.maximum(jnp.abs(den), jnp.exp(-m_t))

        m_l = m_t[lc - 1:lc, :]
        b_l = b_col[lc - 1:lc, :]
        w_l = jnp.exp(b_l - b_col + ig_col - m_l)
        w0_l = jnp.exp(b_l + m0 - m_l)
        c_ref[0, h] = w0_l * c0 + _dot((w_l * vf).T.astype(BF16), kb)
        n_ref[0, h:h + 1, :] = w0_l * n0 + jnp.sum(w_l * kf, axis=0, keepdims=True)
        m_ref[0, h:h + 1, :] = jnp.broadcast_to(m_l, (1, LANES))

        hn = hh * lax.rsqrt(jnp.mean(hh * hh, axis=-1, keepdims=True) + NORM_EPS)
        hn = hn * ln_ref[:, hs] * jax.nn.sigmoid(o_ref[:, hs])
        hn_ref[:, hs] = hn.astype(hn_ref.dtype)


def _mlstm(qk, v, o, gates, conv0, c0, n0, m0, conv_w, conv_b, gate_b, lstm_norm, batch, n_chunks, l_valid):
    assert CONV_WIDTH - 1 <= l_valid <= LSTM_CHUNK
    nc = n_chunks
    tok = lambda b, c: (b * nc + c, 0)
    per_b3 = lambda b, c: (b, 0, 0)
    const = lambda b, c: (0, 0)
    rows = batch * nc * LSTM_CHUNK
    return pl.pallas_call(
        functools.partial(_mlstm_body, l_valid),
        grid=(batch, nc),
        in_specs=[pl.BlockSpec((LSTM_CHUNK, 2 * LSTM_WIDTH), tok),
                  pl.BlockSpec((LSTM_CHUNK, LSTM_WIDTH), tok),
                  pl.BlockSpec((LSTM_CHUNK, LSTM_WIDTH), tok),
                  pl.BlockSpec((LSTM_CHUNK, LANES), tok),
                  pl.BlockSpec((SUBLANES, 2 * LSTM_WIDTH), lambda b, c: (b, 0)),
                  pl.BlockSpec((1, LSTM_HEADS, LSTM_HEAD_DIM, LSTM_HEAD_DIM), lambda b, c: (b, 0, 0, 0)),
                  pl.BlockSpec((1, SUBLANES, LANES), per_b3),
                  pl.BlockSpec((1, SUBLANES, LANES), per_b3),
                  pl.BlockSpec((CONV_WIDTH, 2 * LSTM_WIDTH), const),
                  pl.BlockSpec((1, 2 * LSTM_WIDTH), const),
                  pl.BlockSpec((1, LANES), const),
                  pl.BlockSpec((1, LSTM_WIDTH), const)],
        out_specs=[pl.BlockSpec((LSTM_CHUNK, LSTM_WIDTH), tok),
                   pl.BlockSpec((1, CONV_WIDTH - 1, 2 * LSTM_WIDTH), per_b3),
                   pl.BlockSpec((1, LSTM_HEADS, LSTM_HEAD_DIM, LSTM_HEAD_DIM), lambda b, c: (b, 0, 0, 0)),
                   pl.BlockSpec((1, SUBLANES, LANES), per_b3),
                   pl.BlockSpec((1, SUBLANES, LANES), per_b3)],
        out_shape=[jax.ShapeDtypeStruct((rows, LSTM_WIDTH), BF16),
                   jax.ShapeDtypeStruct((batch, CONV_WIDTH - 1, 2 * LSTM_WIDTH), F32),
                   jax.ShapeDtypeStruct((batch, LSTM_HEADS, LSTM_HEAD_DIM, LSTM_HEAD_DIM), F32),
                   jax.ShapeDtypeStruct((batch, SUBLANES, LANES), F32),
                   jax.ShapeDtypeStruct((batch, SUBLANES, LANES), F32)],
        scratch_shapes=[pltpu.VMEM((LSTM_CHUNK + 2 * SUBLANES, 2 * LSTM_WIDTH), F32)],
        compiler_params=_params(("arbitrary", "arbitrary")),
        name="mlstm",
    )(qk, v, o, gates, conv0, c0, n0, m0, conv_w, conv_b, gate_b, lstm_norm)


def _back_body(n_p, ap_ref, as_ref, hp_ref, hs_ref, xp_ref, xs_ref, woa_ref, wob_ref, g_ref, wr_ref, br_ref,
               x1_ref, h2_ref, rt_ref):
    i = pl.program_id(0)
    is_p = i < n_p
    att = jnp.where(is_p, ap_ref[...], as_ref[...])
    hn = jnp.where(is_p, hp_ref[...], hs_ref[...])
    x = jnp.where(is_p, xp_ref[...], xs_ref[...])
    x1 = x + _dot(att, woa_ref[...]) + _dot(hn, wob_ref[...])
    x1_ref[...] = x1
    ms = jnp.mean(x1 * x1, axis=-1, keepdims=True)
    h2 = (x1 * lax.rsqrt(ms + NORM_EPS) * g_ref[...]).astype(BF16)
    h2_ref[...] = h2
    logits = _dot(h2, wr_ref[...]) + br_ref[...]
    lane = lax.broadcasted_iota(jnp.int32, logits.shape, 1).astype(F32)
    cur = logits
    vals, ids = [], []
    for _ in range(TOP_K):
        mk = jnp.max(cur, axis=-1, keepdims=True)
        ik = jnp.min(jnp.where(cur == mk, lane, float(LANES)), axis=-1, keepdims=True)
        cur = jnp.where(lane == ik, -3e38, cur)
        vals.append(mk)
        ids.append(ik)
    es = [jnp.exp(v - vals[0]) for v in vals]
    tot = es[0]
    for e in es[1:]:
        tot = tot + e
    out = jnp.zeros(logits.shape, F32)
    for k in range(TOP_K):
        out = jnp.where(lane == float(k), ids[k], out)
        out = jnp.where(lane == float(TOP_K + k), es[k] / tot, out)
    rt_ref[...] = out


def _back(att_p, att_s, hn_p, hn_s, xp, xs, wo_a, wo_b, g, w_r, b_r):
    t_p, t_s = xp.shape[0], xs.shape[0]
    tm = t_s
    n_p = t_p // tm
    t_all = t_p + t_s
    row = lambda i: (i, 0)
    prow = lambda i: (jnp.minimum(i, n_p - 1), 0)
    const = lambda i: (0, 0)
    return pl.pallas_call(
        functools.partial(_back_body, n_p),
        grid=(n_p + 1,),
        in_specs=[pl.BlockSpec((tm, ATT_WIDTH), prow), pl.BlockSpec((tm, ATT_WIDTH), const),
                  pl.BlockSpec((tm, LSTM_WIDTH), prow), pl.BlockSpec((tm, LSTM_WIDTH), const),
                  pl.BlockSpec((tm, D_MODEL), prow), pl.BlockSpec((tm, D_MODEL), const),
                  pl.BlockSpec((ATT_WIDTH, D_MODEL), const), pl.BlockSpec((LSTM_WIDTH, D_MODEL), const),
                  pl.BlockSpec((1, D_MODEL), const), pl.BlockSpec((D_MODEL, LANES), const),
                  pl.BlockSpec((1, LANES), const)],
        out_specs=[pl.BlockSpec((tm, D_MODEL), row), pl.BlockSpec((tm, D_MODEL), row),
                   pl.BlockSpec((tm, LANES), row)],
        out_shape=[jax.ShapeDtypeStruct((t_all, D_MODEL), F32), jax.ShapeDtypeStruct((t_all, D_MODEL), BF16),
                   jax.ShapeDtypeStruct((t_all, LANES), F32)],
        compiler_params=_params(("arbitrary",)),
        name="back",
    )(att_p, att_s, hn_p, hn_s, xp, xs, wo_a, wo_b, g, w_r, b_r)


def _expert_body(be_ref, nu_ref, x_ref, wu_ref, bu_ref, wd_ref, bd_ref, y_ref, wu_bf, wd_bf):
    i = pl.program_id(0)
    live = i < nu_ref[0]

    @pl.when(live & ((i == 0) | (be_ref[i] != be_ref[jnp.maximum(i - 1, 0)])))
    def _():
        wu_bf[...] = wu_ref[0].astype(BF16)
        wd_bf[...] = wd_ref[0].astype(BF16)

    @pl.when(live)
    def _():
        xb = x_ref[...]
        acc = None
        for c in range(D_FF // COL_CHUNK):
            lo, hi = c * COL_CHUNK, (c + 1) * COL_CHUNK
            glu = _dot(xb, wu_bf[:, lo:hi]) + bu_ref[0, :, lo:hi]
            lin = _dot(xb, wu_bf[:, D_FF + lo:D_FF + hi]) + bu_ref[0, :, D_FF + lo:D_FF + hi]
            glu = jnp.minimum(glu, SWIGLU_LIMIT)
            lin = jnp.clip(lin, -SWIGLU_LIMIT, SWIGLU_LIMIT)
            mid = (glu * jax.nn.sigmoid(SWIGLU_ALPHA * glu) * (lin + 1.0)).astype(BF16)
            part = _dot(mid, wd_bf[lo:hi, :])
            acc = part if acc is None else acc + part
        y_ref[...] = (acc + bd_ref[0]).astype(y_ref.dtype)

    @pl.when(jnp.logical_not(live))
    def _():
        y_ref[...] = jnp.zeros(y_ref.shape, y_ref.dtype)


def _experts(xs, blk_expert, n_used, w_up, b_up, w_down, b_down):
    cap = xs.shape[0]
    n_blocks = cap // EXPERT_ROWS
    live = lambda i, be, nu: jnp.minimum(i, nu[0] - 1)
    return pl.pallas_call(
        _expert_body,
        grid_spec=pltpu.PrefetchScalarGridSpec(
            num_scalar_prefetch=2,
            grid=(n_blocks,),
            in_specs=[pl.BlockSpec((EXPERT_ROWS, D_MODEL), lambda i, be, nu: (live(i, be, nu), 0)),
                      pl.BlockSpec((1, D_MODEL, 2 * D_FF), lambda i, be, nu: (be[live(i, be, nu)], 0, 0)),
                      pl.BlockSpec((1, 1, 2 * D_FF), lambda i, be, nu: (be[live(i, be, nu)], 0, 0)),
                      pl.BlockSpec((1, D_FF, D_MODEL), lambda i, be, nu: (be[live(i, be, nu)], 0, 0)),
                      pl.BlockSpec((1, 1, D_MODEL), lambda i, be, nu: (be[live(i, be, nu)], 0, 0))],
            out_specs=pl.BlockSpec((EXPERT_ROWS, D_MODEL), lambda i, be, nu: (i, 0)),
            scratch_shapes=[pltpu.VMEM((D_MODEL, 2 * D_FF), BF16), pltpu.VMEM((D_FF, D_MODEL), BF16)]),
        out_shape=jax.ShapeDtypeStruct((cap, D_MODEL), BF16),
        compiler_params=_params(("arbitrary",), EXPERT_VMEM_LIMIT),
        name="experts",
    )(blk_expert, n_used, xs, w_up, b_up, w_down, b_down)


def _combine_body(n_p, x1_ref, yg_ref, rt_ref, g_ref, yp_ref, ys_ref):
    i = pl.program_id(0)
    y = x1_ref[...]
    rt = rt_ref[...]
    for k in range(TOP_K):
        y = y + yg_ref[k] * rt[:, TOP_K + k:TOP_K + k + 1]
    ms = jnp.mean(y * y, axis=-1, keepdims=True)
    out = y * lax.rsqrt(ms + NORM_EPS) * g_ref[...]

    @pl.when(i < n_p)
    def _():
        yp_ref[...] = out

    @pl.when(i >= n_p)
    def _():
        ys_ref[...] = out


def _combine(x1, yg, rt, g, t_p, t_s):
    tm = min(256, t_s)
    assert t_p % tm == 0 and t_s % tm == 0
    n_p, n_s = t_p // tm, t_s // tm
    row = lambda i: (i, 0)
    return pl.pallas_call(
        functools.partial(_combine_body, n_p),
        grid=(n_p + n_s,),
        in_specs=[pl.BlockSpec((tm, D_MODEL), row), pl.BlockSpec((TOP_K, tm, D_MODEL), lambda i: (0, i, 0)),
                  pl.BlockSpec((tm, LANES), row), pl.BlockSpec((1, D_MODEL), lambda i: (0, 0))],
        out_specs=[pl.BlockSpec((tm, D_MODEL), lambda i: (jnp.minimum(i, n_p - 1), 0)),
                   pl.BlockSpec((tm, D_MODEL), lambda i: (jnp.maximum(i - n_p, 0), 0))],
        out_shape=[jax.ShapeDtypeStruct((t_p, D_MODEL), F32), jax.ShapeDtypeStruct((t_s, D_MODEL), F32)],
        compiler_params=_params(("arbitrary",)),
        name="combine",
    )(x1, yg, rt, g)


def _route(top_idx):
    t = top_idx.shape[0]
    n = t * TOP_K
    bm = EXPERT_ROWS
    e_flat = top_idx.reshape(-1)
    e_sorted, order = lax.sort((e_flat, jnp.arange(n, dtype=jnp.int32)), num_keys=1, is_stable=True)
    experts = jnp.arange(N_EXPERTS, dtype=jnp.int32)
    counts = jnp.sum((e_flat[:, None] == experts[None, :]).astype(jnp.int32), axis=0)
    padded = ((counts + bm - 1) // bm) * bm
    pad_end = jnp.cumsum(padded)
    pad_start = pad_end - padded
    grp_start = jnp.cumsum(counts) - counts
    n_blocks = -(-n // bm) + N_EXPERTS
    blk_start = jnp.arange(n_blocks, dtype=jnp.int32) * bm
    blk_expert = jnp.minimum(jnp.sum((pad_end[None, :] <= blk_start[:, None]).astype(jnp.int32), axis=1),
                             N_EXPERTS - 1)
    n_used = (pad_end[-1] // bm).astype(jnp.int32).reshape(1)
    shift = (pad_start - grp_start).astype(jnp.int32)
    row_shift = jnp.broadcast_to(shift[blk_expert][:, None], (n_blocks, bm)).reshape(-1)
    row_pos = jnp.clip(jnp.arange(n_blocks * bm, dtype=jnp.int32) - row_shift, 0, n - 1)
    row_tok = order[row_pos] // TOP_K
    shift_sorted = jnp.sum(jnp.where(e_sorted[:, None] == experts[None, :], shift[None, :], 0), axis=1)
    dest_sorted = jnp.arange(n, dtype=jnp.int32) + shift_sorted
    _, dest = lax.sort((order, dest_sorted), num_keys=1)
    return row_tok, blk_expert, n_used, dest.reshape(t, TOP_K).T.reshape(-1)


def kernel(x_prompt, x_sample, cache_k, cache_v, page_table, state_conv, state_C, state_n, state_m,
           norm_mix, w_in, conv_w, conv_b, gate_b, lstm_norm, w_out, norm_ffn, w_router, b_router,
           w_up, b_up, w_down, b_down, norm_final):
    assert w_in.shape[0] == 1, "single-layer stack"
    batch, seq, _ = x_prompt.shape
    db, ds, _ = x_sample.shape
    t_p, t_s = batch * seq, db * ds
    n_pool = cache_k.shape[1]
    assert seq % LSTM_CHUNK == 0 and page_table.shape[1] * PAGE_SIZE % MOBA_BLOCK == 0

    xp = x_prompt.reshape(t_p, D_MODEL)
    xs = x_sample.reshape(t_s, D_MODEL)
    w_main = w_in[0, :, :MAIN_COLS].astype(BF16)
    w_gate = jnp.pad(w_in[0, :, MAIN_COLS:], ((0, 0), (0, LANES - 2 * LSTM_HEADS))).astype(BF16)
    q_a, k_a, v_a, qk_src, v_l, o_l, gts = _front(xp, xs, norm_mix, w_main, w_gate)

    att_p = _moba(q_a, k_a, v_a, batch, seq)
    k_s, v_s = k_a[t_p:], v_a[t_p:]

    def key_minor(cache):
        cache = cache.reshape(n_pool, ATT_HEADS, PAGE_SIZE, ATT_HEAD_DIM).transpose(0, 1, 3, 2)
        return cache.reshape(n_pool, ATT_WIDTH, PAGE_SIZE)

    att_s = _samp_attn(q_a[t_p:], k_s, v_s, key_minor(cache_k), key_minor(cache_v), page_table, ds).astype(BF16)

    gate_b_row = jnp.pad(gate_b, ((0, 0), (0, LANES - 2 * LSTM_HEADS)))
    cw, cb = conv_w[0], conv_b
    hn_p, conv_p, c_p, n_p_, m_p = _mlstm(
        qk_src, v_l, o_l, gts,
        jnp.zeros((batch * SUBLANES, 2 * LSTM_WIDTH), F32),
        jnp.zeros((batch, LSTM_HEADS, LSTM_HEAD_DIM, LSTM_HEAD_DIM), F32),
        jnp.zeros((batch, SUBLANES, LANES), F32), jnp.zeros((batch, SUBLANES, LANES), F32),
        cw, cb, gate_b_row, lstm_norm, batch, seq // LSTM_CHUNK, LSTM_CHUNK)

    def pad_chunk(a):
        a = a[t_p:].reshape(db, ds, a.shape[-1])
        return jnp.pad(a, ((0, 0), (0, LSTM_CHUNK - ds), (0, 0))).reshape(db * LSTM_CHUNK, a.shape[-1])

    conv0_s = jnp.pad(state_conv[0], ((0, 0), (SUBLANES - (CONV_WIDTH - 1), 0), (0, 0)))
    n0_s = jnp.pad(state_n[0], ((0, 0), (0, SUBLANES - LSTM_HEADS), (0, 0)))
    m0_s = jnp.broadcast_to(jnp.pad(state_m[0], ((0, 0), (0, SUBLANES - LSTM_HEADS)))[:, :, None],
                            (db, SUBLANES, LANES))
    hn_s, conv_s, c_s, n_s, m_s = _mlstm(
        pad_chunk(qk_src), pad_chunk(v_l), pad_chunk(o_l), pad_chunk(gts),
        conv0_s.reshape(db * SUBLANES, 2 * LSTM_WIDTH), state_C[0], n0_s, m0_s,
        cw, cb, gate_b_row, lstm_norm, db, 1, ds)
    hn_s = hn_s.reshape(db, LSTM_CHUNK, LSTM_WIDTH)[:, :ds].reshape(t_s, LSTM_WIDTH)

    wo = w_out[0].astype(BF16)
    w_r = jnp.pad(w_router[0], ((0, 0), (0, LANES - N_EXPERTS))).astype(BF16)
    b_r = jnp.pad(b_router, ((0, 0), (0, LANES - N_EXPERTS)), constant_values=NEG)
    x1, h2, rt = _back(att_p, att_s, hn_p, hn_s, xp, xs, wo[:ATT_WIDTH], wo[ATT_WIDTH:], norm_ffn, w_r, b_r)

    row_tok, blk_expert, n_used, dest = _route(rt[:, :TOP_K].astype(jnp.int32))
    ys = _experts(h2[row_tok], blk_expert, n_used, w_up[0], b_up[0][:, None, :], w_down[0], b_down[0][:, None, :])
    yg = ys[dest].reshape(TOP_K, t_p + t_s, D_MODEL)
    y_p, y_s = _combine(x1, yg, rt, norm_final.reshape(1, D_MODEL), t_p, t_s)

    def to_pages(t):
        t = t[:t_p].reshape(batch, seq // PAGE_SIZE, PAGE_SIZE, ATT_HEADS, ATT_HEAD_DIM)
        return t.transpose(0, 1, 3, 2, 4)[None]

    def to_slots(t):
        return t.reshape(db, ds, ATT_HEADS, ATT_HEAD_DIM).transpose(0, 2, 1, 3)[None]

    return (y_p.reshape(batch, seq, D_MODEL), y_s.reshape(db, ds, D_MODEL),
            to_pages(k_a), to_pages(v_a), to_slots(k_s), to_slots(v_s),
            conv_p[None], conv_s[None],
            c_p[None], n_p_[None, :, :LSTM_HEADS, :], m_p[None, :, :LSTM_HEADS, 0],
            c_s[None], n_s[None, :, :LSTM_HEADS, :], m_s[None, :, :LSTM_HEADS, 0])
```

```python
import functools

import jax
import jax.numpy as jnp
from jax import lax
from jax.experimental import pallas as pl
from jax.experimental.pallas import tpu as pltpu

F32 = jnp.float32
BF16 = jnp.bfloat16

D_MODEL = 1024
ATT_HEADS = 8
ATT_HEAD_DIM = 64
ATT_WIDTH = ATT_HEADS * ATT_HEAD_DIM
MOBA_BLOCK = 256
MOBA_TOPK = 3
LSTM_HEADS = 4
LSTM_HEAD_DIM = 128
LSTM_WIDTH = LSTM_HEADS * LSTM_HEAD_DIM
CONV_WIDTH = 4
LSTM_CHUNK = 128
PAGE_SIZE = 128
N_EXPERTS = 32
TOP_K = 4
D_FF = D_MODEL
SWIGLU_LIMIT = 7.0
SWIGLU_ALPHA = 1.702
NORM_EPS = 1e-5
MAIN_COLS = 3 * ATT_WIDTH + 4 * LSTM_WIDTH
COL_CHUNK = 512
LANES = 128
SUBLANES = 8
EXPERT_ROWS = 512
NEG = -1e30
LOG2E = 1.4426950408889634
N_FEAT = 4
SCORE_AHEAD = 4
VMEM_LIMIT = 48 * 1024 * 1024
EXPERT_VMEM_LIMIT = 56 * 1024 * 1024

_NT = (((1,), (1,)), ((), ()))
_HI = lax.Precision.HIGHEST


def _dot(a, b):
    return jnp.dot(a, b, preferred_element_type=F32)


def _dot_nt(a, b, precision=None):
    return lax.dot_general(a, b, _NT, precision=precision, preferred_element_type=F32)


def _params(sem, vmem_limit=VMEM_LIMIT):
    return pltpu.CompilerParams(dimension_semantics=sem, vmem_limit_bytes=vmem_limit)


def _front_body(n_p, xp_ref, xs_ref, g_ref, w_ref, wg_ref,
                qa_ref, ka_ref, va_ref, qk_ref, vl_ref, ol_ref, gt_ref, kpg_ref, vpg_ref):
    i = pl.program_id(0)
    x = jnp.where(i < n_p, xp_ref[...], xs_ref[...])
    ms = jnp.mean(x * x, axis=-1, keepdims=True)
    h = (x * lax.rsqrt(ms + NORM_EPS) * g_ref[...]).astype(BF16)
    dests = ((qa_ref, 0, None), (ka_ref, 0, kpg_ref), (va_ref, 0, vpg_ref), (qk_ref, 0, None), (qk_ref, 1, None),
             (vl_ref, 0, None), (ol_ref, 0, None))
    for c, (ref, part, pages_ref) in enumerate(dests):
        z = _dot(h, w_ref[:, c * COL_CHUNK:(c + 1) * COL_CHUNK])
        ref[:, part * COL_CHUNK:(part + 1) * COL_CHUNK] = z.astype(ref.dtype)
        if pages_ref is not None:

            @pl.when(i < n_p)
            def _(z=z, pages_ref=pages_ref):
                for pg in range(pages_ref.shape[0]):
                    pages_ref[pg] = z[pg * PAGE_SIZE:(pg + 1) * PAGE_SIZE, :].T
    gt_ref[...] = _dot(h, wg_ref[...])


def _front(xp, xs, g, w_main, w_gate):
    t_p, t_s = xp.shape[0], xs.shape[0]
    tm = t_s
    assert t_p % tm == 0 and tm % 128 == 0
    n_p = t_p // tm
    t_all = t_p + t_s
    row = lambda i: (i, 0)
    const = lambda i: (0, 0)
    wide = lambda n, dt: jax.ShapeDtypeStruct((t_all, n), dt)
    ppt = tm // PAGE_SIZE
    pages = pl.BlockSpec((ppt, ATT_WIDTH, PAGE_SIZE), lambda i: (jnp.minimum(i, n_p - 1), 0, 0))
    pages_shape = jax.ShapeDtypeStruct((t_p // PAGE_SIZE, ATT_WIDTH, PAGE_SIZE), F32)
    return pl.pallas_call(
        functools.partial(_front_body, n_p),
        grid=(n_p + 1,),
        in_specs=[pl.BlockSpec((tm, D_MODEL), lambda i: (jnp.minimum(i, n_p - 1), 0)),
                  pl.BlockSpec((tm, D_MODEL), const),
                  pl.BlockSpec((1, D_MODEL), const),
                  pl.BlockSpec((D_MODEL, MAIN_COLS), const),
                  pl.BlockSpec((D_MODEL, LANES), const)],
        out_specs=[pl.BlockSpec((tm, ATT_WIDTH), row), pl.BlockSpec((tm, ATT_WIDTH), row),
                   pl.BlockSpec((tm, ATT_WIDTH), row), pl.BlockSpec((tm, 2 * LSTM_WIDTH), row),
                   pl.BlockSpec((tm, LSTM_WIDTH), row), pl.BlockSpec((tm, LSTM_WIDTH), row),
                   pl.BlockSpec((tm, LANES), row), pages, pages],
        out_shape=[wide(ATT_WIDTH, BF16), wide(ATT_WIDTH, F32), wide(ATT_WIDTH, F32),
                   wide(2 * LSTM_WIDTH, F32), wide(LSTM_WIDTH, F32), wide(LSTM_WIDTH, F32),
                   wide(LANES, F32), pages_shape, pages_shape],
        compiler_params=_params(("arbitrary",)),
        name="front",
    )(xp, xs, g, w_main, w_gate)


def _topk_rows(g, n_valid, k_sel):
    n = g.shape[0]
    jj = lax.broadcasted_iota(jnp.int32, g.shape, 0)
    rank = jnp.zeros(g.shape, F32)
    for j2 in range(n):
        row = g[j2:j2 + 1, :]
        beats = jnp.where(row > g, 1.0, jnp.where((row == g) & (jj > j2), 1.0, 0.0))
        rank = rank + jnp.where(j2 < n_valid, beats, 0.0)
    return jnp.where((jj < n_valid) & (rank < k_sel), 1.0, 0.0)


def _topk_lanes(g, n, k_sel):
    jj = lax.broadcasted_iota(jnp.int32, g.shape, 1)
    rank = jnp.zeros(g.shape, F32)
    for j2 in range(n):
        col = g[:, j2:j2 + 1]
        rank = rank + jnp.where(col > g, 1.0, jnp.where((col == g) & (jj > j2), 1.0, 0.0))
    return jnp.where((jj < n) & (rank < k_sel), 1.0, 0.0)


def _moba_body(nb, q_ref, k_ref, v_ref, o_ref, kaug, vt, kmean):
    i = pl.program_id(1)
    blk = MOBA_BLOCK
    hd = ATT_HEAD_DIM
    seq = nb * blk
    k_sel = min(MOBA_TOPK, nb - 1)
    heads_per = LANES // hd

    @pl.when(i == 0)
    def _():
        row = lax.broadcasted_iota(jnp.int32, (seq, LANES), 0)
        lane_s = lax.broadcasted_iota(jnp.int32, (seq, LANES), 1)
        for c in range(ATT_WIDTH // LANES):
            kf = k_ref[:, c * LANES:(c + 1) * LANES]
            kmean[c] = jnp.sum(kf.reshape(nb, blk, LANES), axis=1) * (1.0 / blk)
            for hl in range(heads_per):
                f = lane_s - (heads_per - 1 - hl) * hd
                feat = jnp.where(f == 0, (row % blk).astype(F32),
                                 jnp.where(f == 1, (row // blk * blk).astype(F32),
                                           jnp.where((f == 2) | (f == 3), 1.0,
                                                     jnp.where(f - N_FEAT == row // blk, 1.0, 0.0))))
                kaug[c * heads_per + hl] = jnp.where((lane_s // hd) == hl, kf, feat).astype(BF16)
        for jb in range(nb):
            vt[jb] = v_ref[jb * blk:(jb + 1) * blk, :].T.astype(BF16)

    key_io = lax.broadcasted_iota(jnp.int32, (blk, blk), 0)
    qry_io = lax.broadcasted_iota(jnp.int32, (blk, blk), 1)
    causal = key_io <= qry_io
    q_row = lax.broadcasted_iota(jnp.int32, (blk, LANES), 0)
    lane = lax.broadcasted_iota(jnp.int32, (blk, LANES), 1)
    blk_io = lax.broadcasted_iota(jnp.int32, (nb, blk), 0)
    own = pl.multiple_of(i * blk, blk)

    q_augs = []
    for hp in range(ATT_WIDTH // LANES):
        sl = slice(hp * LANES, (hp + 1) * LANES)
        q2 = q_ref[:, sl].astype(F32) * (hd ** -0.5)
        km = kmean[hp]
        for hl in range(heads_per):
            head = hp * heads_per + hl
            slope = float(2.0 ** (-8.0 * (head + 1) / ATT_HEADS))
            inhead = (lane // hd) == hl
            gate_t = _dot_nt(km, jnp.where(inhead, q2, 0.0), _HI)
            keep = (_topk_rows(gate_t, i, k_sel) > 0.5) | (blk_io == i)
            spare = (heads_per - 1 - hl) * hd + N_FEAT
            mask_t = jnp.concatenate([jnp.zeros((spare, blk), F32), jnp.where(keep, 0.0, NEG),
                                      jnp.zeros((LANES - spare - nb, blk), F32)], axis=0)
            f = lane - (heads_per - 1 - hl) * hd
            feat = jnp.where(f <= 1, slope,
                             jnp.where(f == 2, -slope * (i * blk).astype(F32),
                                       jnp.where(f == 3, -slope * q_row.astype(F32), mask_t.T)))
            q_augs.append(jnp.where(inhead, q2, feat).astype(BF16))

    def block_pass(j, off, carry):
        def scores(h):
            return _dot_nt(kaug[h, pl.ds(off, blk), :], q_augs[h]) * LOG2E

        ahead = {h: scores(h) for h in range(min(SCORE_AHEAD, ATT_HEADS))}
        out = ()
        for h in range(ATT_HEADS):
            t = ahead.pop(h)
            if carry is None:
                t = jnp.where(causal, t, NEG)
                m_new = jnp.max(t, axis=0, keepdims=True)
            else:
                m, l, acc = carry[3 * h:3 * h + 3]
                m_new = jnp.maximum(m, jnp.max(t, axis=0, keepdims=True))
            p = jnp.exp2(t - m_new)
            if h + SCORE_AHEAD < ATT_HEADS:
                ahead[h + SCORE_AHEAD] = scores(h + SCORE_AHEAD)
            pv = _dot(vt[j, h * hd:(h + 1) * hd, :], p.astype(BF16))
            if carry is None:
                out = out + (m_new, jnp.sum(p, axis=0, keepdims=True), pv)
            else:
                a = jnp.exp2(m - m_new)
                out = out + (m_new, a * l + jnp.sum(p, axis=0, keepdims=True), a * acc + pv)
        return out

    carry = block_pass(i, own, None)
    carry = lax.fori_loop(0, i, lambda j, c: block_pass(j, pl.multiple_of(j * blk, blk), c), carry)
    for hp in range(ATT_WIDTH // LANES):
        o_t = jnp.concatenate([carry[3 * h + 2] / carry[3 * h + 1]
                               for h in range(hp * heads_per, (hp + 1) * heads_per)], axis=0)
        o_ref[:, hp * LANES:(hp + 1) * LANES] = o_t.T.astype(o_ref.dtype)


def _moba(q_all, k_all, v_all, batch, seq):
    nb = seq // MOBA_BLOCK
    assert seq % MOBA_BLOCK == 0 and N_FEAT + nb <= ATT_HEAD_DIM and ATT_HEADS == 8
    return pl.pallas_call(
        functools.partial(_moba_body, nb),
        grid=(batch, nb),
        in_specs=[pl.BlockSpec((MOBA_BLOCK, ATT_WIDTH), lambda b, i: (b * nb + i, 0)),
                  pl.BlockSpec((seq, ATT_WIDTH), lambda b, i: (b, 0)),
                  pl.BlockSpec((seq, ATT_WIDTH), lambda b, i: (b, 0))],
        out_specs=pl.BlockSpec((MOBA_BLOCK, ATT_WIDTH), lambda b, i: (b * nb + i, 0)),
        out_shape=jax.ShapeDtypeStruct((batch * seq, ATT_WIDTH), BF16),
        scratch_shapes=[pltpu.VMEM((ATT_HEADS, seq, LANES), BF16), pltpu.VMEM((nb, ATT_WIDTH, MOBA_BLOCK), BF16),
                        pltpu.VMEM((ATT_WIDTH // LANES, nb, LANES), F32)],
        compiler_params=_params(("arbitrary", "arbitrary")),
        name="moba",
    )(q_all, k_all, v_all)


def _samp_attn_body(n_blk, bps, past_len, dec_seq, pt_ref, qbd_ref, q_ref, kn_ref, vn_ref, *rest):
    del pt_ref
    ppb = MOBA_BLOCK // PAGE_SIZE
    k_refs, v_refs = rest[:ppb * bps], rest[ppb * bps:2 * ppb * bps]
    o_ref, m_scr, l_scr, acc_scr, g_scr = rest[2 * ppb * bps:]
    step = pl.program_id(1)
    hq = ATT_HEADS * SUBLANES
    scale = ATT_HEAD_DIM ** -0.5
    qbd = qbd_ref[0]
    wide = bps * MOBA_BLOCK
    rows = lax.broadcasted_iota(jnp.int32, (hq, wide), 0)
    slope = jnp.exp2((rows // SUBLANES + 1).astype(F32) * (-8.0 / ATT_HEADS))
    tq = (past_len + rows % SUBLANES).astype(F32)
    pos = (step * wide + lax.broadcasted_iota(jnp.int32, (hq, wide), 1)).astype(F32)
    lanes = lax.broadcasted_iota(jnp.int32, (hq, LANES), 1)

    @pl.when(step == 0)
    def _():
        g_scr[...] = jnp.zeros(g_scr.shape, F32)
        m_scr[...] = jnp.zeros(m_scr.shape, F32)
        l_scr[...] = jnp.zeros(l_scr.shape, F32)

    k_cat = jnp.concatenate([r[...].astype(BF16) for r in k_refs], axis=1)
    s_raw = _dot(qbd, k_cat)
    s = s_raw * scale - slope * (tq - pos)
    ones = jnp.ones((MOBA_BLOCK, LANES), BF16)
    g_all, m_all, l_all = g_scr[...], m_scr[...], l_scr[...]
    for bl in range(bps):
        j = step * bps + bl
        seg = slice(bl * MOBA_BLOCK, (bl + 1) * MOBA_BLOCK)
        m = jnp.max(s[:, seg], axis=-1, keepdims=True)
        p = jnp.exp(s[:, seg] - m).astype(BF16)
        l_rep = _dot(p, ones)
        v_cat = jnp.concatenate([v_refs[ppb * bl + pg][...].astype(BF16) for pg in range(ppb)], axis=1)
        acc_scr[j] = _dot_nt(p, v_cat)
        gate = jnp.sum(s_raw[:, seg], axis=-1, keepdims=True) * (1.0 / MOBA_BLOCK)
        hit = lanes == j
        g_all = jnp.where(hit, gate, g_all)
        m_all = jnp.where(hit, m, m_all)
        l_all = jnp.where(hit, l_rep, l_all)
    g_scr[...] = g_all
    m_scr[...] = m_all
    l_scr[...] = l_all

    @pl.when(step == n_blk // bps - 1)
    def _():
        chosen = _topk_lanes(g_all, n_blk, min(MOBA_TOPK, n_blk)) > 0.5

        r2 = lax.broadcasted_iota(jnp.int32, (hq, hq), 0)
        c2 = lax.broadcasted_iota(jnp.int32, (hq, hq), 1)
        ok = ((c2 // SUBLANES) == (r2 // SUBLANES)) & ((c2 % SUBLANES) <= (r2 % SUBLANES)) \
            & ((c2 % SUBLANES) < dec_seq)
        slope2 = jnp.exp2((r2 // SUBLANES + 1).astype(F32) * (-8.0 / ATT_HEADS))
        s = _dot_nt(q_ref[0], kn_ref[0]) * scale - slope2 * (r2 % SUBLANES - c2 % SUBLANES).astype(F32)
        s = jnp.where(ok, s, NEG)
        m_loc = jnp.max(s, axis=-1, keepdims=True)
        p = jnp.exp(s - m_loc)
        l_loc = jnp.sum(p, axis=-1, keepdims=True)
        a_loc = _dot(p.astype(BF16), vn_ref[0])

        m_tot = jnp.maximum(m_loc, jnp.max(jnp.where(chosen, m_all, NEG), axis=-1, keepdims=True))
        w_blk = jnp.where(chosen, jnp.exp(m_all - m_tot), 0.0)
        w_loc = jnp.exp(m_loc - m_tot)
        den = w_loc * l_loc + jnp.sum(w_blk * l_all, axis=-1, keepdims=True)
        num_w = jnp.zeros((hq, ATT_WIDTH), F32)
        for jb in range(n_blk):
            num_w = num_w + w_blk[:, jb:jb + 1] * acc_scr[jb]
        num = jnp.concatenate(
            [num_w[h * SUBLANES:(h + 1) * SUBLANES, h * ATT_HEAD_DIM:(h + 1) * ATT_HEAD_DIM]
             for h in range(ATT_HEADS)], axis=0)
        o_ref[0] = (num + w_loc * a_loc) / den


def _samp_attn(q_s, k_s, v_s, cache_kt, cache_vt, page_table, dec_seq):
    db, n_pages = page_table.shape
    ppb = MOBA_BLOCK // PAGE_SIZE
    assert ppb == 2 and n_pages % ppb == 0 and dec_seq <= SUBLANES
    n_blk = n_pages // ppb
    assert 1 <= n_blk <= LANES
    bps = next(c for c in (8, 4, 2, 1) if n_blk % c == 0)
    past_len = n_pages * PAGE_SIZE
    hq = ATT_HEADS * SUBLANES

    def rows(t):
        t = t.reshape(db, dec_seq, ATT_HEADS, ATT_HEAD_DIM).transpose(0, 2, 1, 3)
        return jnp.pad(t, ((0, 0), (0, 0), (0, SUBLANES - dec_seq), (0, 0)))

    flat = lambda t: t.reshape(db, hq, ATT_HEAD_DIM).astype(BF16)
    q_r = rows(q_s)
    eye = jnp.eye(ATT_HEADS, dtype=q_r.dtype)
    q_bd = (q_r[:, :, :, None, :] * eye[None, :, None, :, None]).reshape(db, hq, ATT_WIDTH).astype(BF16)

    small = pl.BlockSpec((1, hq, ATT_HEAD_DIM), lambda b, s, pt: (b, 0, 0))

    def page(k):
        return pl.BlockSpec((None, ATT_WIDTH, PAGE_SIZE),
                            lambda b, s, pt: (pt[b * n_pages + ppb * bps * s + k], 0, 0))

    pages = [page(k) for k in range(ppb * bps)]
    out = pl.pallas_call(
        functools.partial(_samp_attn_body, n_blk, bps, past_len, dec_seq),
        grid_spec=pltpu.PrefetchScalarGridSpec(
            num_scalar_prefetch=1,
            grid=(db, n_blk // bps),
            in_specs=[pl.BlockSpec((1, hq, ATT_WIDTH), lambda b, s, pt: (b, 0, 0)), small, small, small]
            + pages + pages,
            out_specs=small,
            scratch_shapes=[pltpu.VMEM((hq, LANES), F32), pltpu.VMEM((hq, LANES), F32),
                            pltpu.VMEM((n_blk, hq, ATT_WIDTH), F32), pltpu.VMEM((hq, LANES), F32)]),
        out_shape=jax.ShapeDtypeStruct((db, hq, ATT_HEAD_DIM), F32),
        compiler_params=_params(("arbitrary", "arbitrary")),
        name="samp_attn",
    )(page_table.reshape(-1), q_bd, flat(q_r), flat(rows(k_s)), flat(rows(v_s)),
      *([cache_kt] * (ppb * bps)), *([cache_vt] * (ppb * bps)))
    out = out.reshape(db, ATT_HEADS, SUBLANES, ATT_HEAD_DIM)[:, :, :dec_seq]
    return out.transpose(0, 2, 1, 3).reshape(db * dec_seq, ATT_WIDTH)


def _mlstm_body(l_valid, qk_ref, v_ref, o_ref, g_ref, conv0_ref, c0_ref, n0_ref, m0_ref,
                cw_ref, cb_ref, gb_ref, ln_ref,
                hn_ref, convn_ref, c_ref, n_ref, m_ref, ubuf):
    c = pl.program_id(1)
    lc = LSTM_CHUNK
    hd = LSTM_HEAD_DIM

    @pl.when(c == 0)
    def _():
        ubuf[0:SUBLANES, :] = conv0_ref[...]
        c_ref[...] = c0_ref[...]
        n_ref[...] = n0_ref[...]
        m_ref[...] = m0_ref[...]

    u = qk_ref[...]
    ubuf[SUBLANES:SUBLANES + lc, :] = u
    y = cb_ref[...] + u * cw_ref[CONV_WIDTH - 1:CONV_WIDTH, :]
    for s in range(1, CONV_WIDTH):
        y = y + ubuf[SUBLANES - s:SUBLANES - s + lc, :] * cw_ref[CONV_WIDTH - 1 - s:CONV_WIDTH - s, :]
    qk = y * jax.nn.sigmoid(y)
    convn_ref[0] = ubuf[SUBLANES + l_valid - (CONV_WIDTH - 1):SUBLANES + l_valid, :]
    ubuf[0:SUBLANES, :] = ubuf[lc:lc + SUBLANES, :]

    row = lax.broadcasted_iota(jnp.int32, (lc, lc), 0)
    col = lax.broadcasted_iota(jnp.int32, (lc, lc), 1)
    tri = col <= row
    g = g_ref[...] + gb_ref[...]
    log_f = jnp.minimum(g, 0.0) - jnp.log1p(jnp.exp(-jnp.abs(g)))
    is_f = (col >= LSTM_HEADS) & (col < 2 * LSTM_HEADS)
    gates = jnp.where(is_f, log_f, g)
    if l_valid < lc:
        gates = jnp.where(row < l_valid, gates, jnp.where(is_f, 0.0, NEG))
    csum = jnp.dot(jnp.where(tri, 1.0, 0.0), jnp.where(is_f, gates, 0.0),
                   precision=_HI, preferred_element_type=F32)
    gates_t = gates.T
    csum_t = csum.T

    for h in range(LSTM_HEADS):
        hs = slice(h * hd, (h + 1) * hd)
        qf = qk[:, hs]
        kf = qk[:, LSTM_WIDTH + h * hd:LSTM_WIDTH + (h + 1) * hd] * (hd ** -0.5)
        vf = v_ref[:, hs]
        qb, kb, vb = qf.astype(BF16), kf.astype(BF16), vf.astype(BF16)
        b_col = csum[:, LSTM_HEADS + h:LSTM_HEADS + h + 1]
        b_row = csum_t[LSTM_HEADS + h:LSTM_HEADS + h + 1, :]
        ig_row = gates_t[h:h + 1, :]
        ig_col = gates[:, h:h + 1]
        m0 = m_ref[0, h:h + 1, 0:1]
        n0 = n_ref[0, h:h + 1, :]
        c0 = c_ref[0, h]

        log_d = jnp.where(tri, b_col - b_row + ig_row, NEG)
        dec0 = b_col + m0
        m_t = jnp.maximum(dec0, jnp.max(log_d, axis=-1, keepdims=True))
        d_m = jnp.exp(log_d - m_t)
        w0 = jnp.exp(dec0 - m_t)
        s = _dot_nt(qb, kb) * d_m
        num = w0 * _dot_nt(qb, c0.astype(BF16)) + _dot(s.astype(BF16), vb)
        den = w0 * jnp.sum(qf * n0, axis=-1, keepdims=True) + jnp.sum(s, axis=-1, keepdims=True)
        hh = num / jnp.maximum(jnp.abs(den), jnp.exp(-m_t))

        m_l = m_t[lc - 1:lc, :]
        b_l = b_col[lc - 1:lc, :]
        w_l = jnp.exp(b_l - b_col + ig_col - m_l)
        w0_l = jnp.exp(b_l + m0 - m_l)
        c_ref[0, h] = w0_l * c0 + _dot((w_l * vf).T.astype(BF16), kb)
        n_ref[0, h:h + 1, :] = w0_l * n0 + jnp.sum(w_l * kf, axis=0, keepdims=True)
        m_ref[0, h:h + 1, :] = jnp.broadcast_to(m_l, (1, LANES))

        hn = hh * lax.rsqrt(jnp.mean(hh * hh, axis=-1, keepdims=True) + NORM_EPS)
        hn = hn * ln_ref[:, hs] * jax.nn.sigmoid(o_ref[:, hs])
        hn_ref[:, hs] = hn.astype(hn_ref.dtype)


def _mlstm(qk, v, o, gates, conv0, c0, n0, m0, conv_w, conv_b, gate_b, lstm_norm, batch, n_chunks, l_valid):
    assert CONV_WIDTH - 1 <= l_valid <= LSTM_CHUNK
    nc = n_chunks
    tok = lambda b, c: (b * nc + c, 0)
    per_b3 = lambda b, c: (b, 0, 0)
    const = lambda b, c: (0, 0)
    rows = batch * nc * LSTM_CHUNK
    return pl.pallas_call(
        functools.partial(_mlstm_body, l_valid),
        grid=(batch, nc),
        in_specs=[pl.BlockSpec((LSTM_CHUNK, 2 * LSTM_WIDTH), tok),
                  pl.BlockSpec((LSTM_CHUNK, LSTM_WIDTH), tok),
                  pl.BlockSpec((LSTM_CHUNK, LSTM_WIDTH), tok),
                  pl.BlockSpec((LSTM_CHUNK, LANES), tok),
                  pl.BlockSpec((SUBLANES, 2 * LSTM_WIDTH), lambda b, c: (b, 0)),
                  pl.BlockSpec((1, LSTM_HEADS, LSTM_HEAD_DIM, LSTM_HEAD_DIM), lambda b, c: (b, 0, 0, 0)),
                  pl.BlockSpec((1, SUBLANES, LANES), per_b3),
                  pl.BlockSpec((1, SUBLANES, LANES), per_b3),
                  pl.BlockSpec((CONV_WIDTH, 2 * LSTM_WIDTH), const),
                  pl.BlockSpec((1, 2 * LSTM_WIDTH), const),
                  pl.BlockSpec((1, LANES), const),
                  pl.BlockSpec((1, LSTM_WIDTH), const)],
        out_specs=[pl.BlockSpec((LSTM_CHUNK, LSTM_WIDTH), tok),
                   pl.BlockSpec((1, CONV_WIDTH - 1, 2 * LSTM_WIDTH), per_b3),
                   pl.BlockSpec((1, LSTM_HEADS, LSTM_HEAD_DIM, LSTM_HEAD_DIM), lambda b, c: (b, 0, 0, 0)),
                   pl.BlockSpec((1, SUBLANES, LANES), per_b3),
                   pl.BlockSpec((1, SUBLANES, LANES), per_b3)],
        out_shape=[jax.ShapeDtypeStruct((rows, LSTM_WIDTH), BF16),
                   jax.ShapeDtypeStruct((batch, CONV_WIDTH - 1, 2 * LSTM_WIDTH), F32),
                   jax.ShapeDtypeStruct((batch, LSTM_HEADS, LSTM_HEAD_DIM, LSTM_HEAD_DIM), F32),
                   jax.ShapeDtypeStruct((batch, SUBLANES, LANES), F32),
                   jax.ShapeDtypeStruct((batch, SUBLANES, LANES), F32)],
        scratch_shapes=[pltpu.VMEM((LSTM_CHUNK + 2 * SUBLANES, 2 * LSTM_WIDTH), F32)],
        compiler_params=_params(("arbitrary", "arbitrary")),
        name="mlstm",
    )(qk, v, o, gates, conv0, c0, n0, m0, conv_w, conv_b, gate_b, lstm_norm)


def _back_body(n_p, ap_ref, as_ref, hp_ref, hs_ref, xp_ref, xs_ref, woa_ref, wob_ref, g_ref, wr_ref, br_ref,
               x1_ref, h2_ref, rt_ref):
    i = pl.program_id(0)
    is_p = i < n_p
    att = jnp.where(is_p, ap_ref[...], as_ref[...])
    hn = jnp.where(is_p, hp_ref[...], hs_ref[...])
    x = jnp.where(is_p, xp_ref[...], xs_ref[...])
    x1 = x + _dot(att, woa_ref[...]) + _dot(hn, wob_ref[...])
    x1_ref[...] = x1
    ms = jnp.mean(x1 * x1, axis=-1, keepdims=True)
    h2 = (x1 * lax.rsqrt(ms + NORM_EPS) * g_ref[...]).astype(BF16)
    h2_ref[...] = h2
    logits = _dot(h2, wr_ref[...]) + br_ref[...]
    lane = lax.broadcasted_iota(jnp.int32, logits.shape, 1).astype(F32)
    cur = logits
    vals, ids = [], []
    for _ in range(TOP_K):
        mk = jnp.max(cur, axis=-1, keepdims=True)
        ik = jnp.min(jnp.where(cur == mk, lane, float(LANES)), axis=-1, keepdims=True)
        cur = jnp.where(lane == ik, -3e38, cur)
        vals.append(mk)
        ids.append(ik)
    es = [jnp.exp(v - vals[0]) for v in vals]
    tot = es[0]
    for e in es[1:]:
        tot = tot + e
    out = jnp.zeros(logits.shape, F32)
    for k in range(TOP_K):
        out = jnp.where(lane == float(k), ids[k], out)
        out = jnp.where(lane == float(TOP_K + k), es[k] / tot, out)
    rt_ref[...] = out


def _back(att_p, att_s, hn_p, hn_s, xp, xs, wo_a, wo_b, g, w_r, b_r):
    t_p, t_s = xp.shape[0], xs.shape[0]
    tm = t_s
    n_p = t_p // tm
    t_all = t_p + t_s
    row = lambda i: (i, 0)
    prow = lambda i: (jnp.minimum(i, n_p - 1), 0)
    const = lambda i: (0, 0)
    return pl.pallas_call(
        functools.partial(_back_body, n_p),
        grid=(n_p + 1,),
        in_specs=[pl.BlockSpec((tm, ATT_WIDTH), prow), pl.BlockSpec((tm, ATT_WIDTH), const),
                  pl.BlockSpec((tm, LSTM_WIDTH), prow), pl.BlockSpec((tm, LSTM_WIDTH), const),
                  pl.BlockSpec((tm, D_MODEL), prow), pl.BlockSpec((tm, D_MODEL), const),
                  pl.BlockSpec((ATT_WIDTH, D_MODEL), const), pl.BlockSpec((LSTM_WIDTH, D_MODEL), const),
                  pl.BlockSpec((1, D_MODEL), const), pl.BlockSpec((D_MODEL, LANES), const),
                  pl.BlockSpec((1, LANES), const)],
        out_specs=[pl.BlockSpec((tm, D_MODEL), row), pl.BlockSpec((tm, D_MODEL), row),
                   pl.BlockSpec((tm, LANES), row)],
        out_shape=[jax.ShapeDtypeStruct((t_all, D_MODEL), F32), jax.ShapeDtypeStruct((t_all, D_MODEL), BF16),
                   jax.ShapeDtypeStruct((t_all, LANES), F32)],
        compiler_params=_params(("arbitrary",)),
        name="back",
    )(att_p, att_s, hn_p, hn_s, xp, xs, wo_a, wo_b, g, w_r, b_r)


def _expert_body(be_ref, nu_ref, x_ref, wu_ref, bu_ref, wd_ref, bd_ref, y_ref, wu_bf, wd_bf):
    i = pl.program_id(0)
    live = i < nu_ref[0]

    @pl.when(live & ((i == 0) | (be_ref[i] != be_ref[jnp.maximum(i - 1, 0)])))
    def _():
        wu_bf[...] = wu_ref[0].astype(BF16)
        wd_bf[...] = wd_ref[0].astype(BF16)

    @pl.when(live)
    def _():
        xb = x_ref[...]
        acc = None
        for c in range(D_FF // COL_CHUNK):
            lo, hi = c * COL_CHUNK, (c + 1) * COL_CHUNK
            glu = _dot(xb, wu_bf[:, lo:hi]) + bu_ref[0, :, lo:hi]
            lin = _dot(xb, wu_bf[:, D_FF + lo:D_FF + hi]) + bu_ref[0, :, D_FF + lo:D_FF + hi]
            glu = jnp.minimum(glu, SWIGLU_LIMIT)
            lin = jnp.clip(lin, -SWIGLU_LIMIT, SWIGLU_LIMIT)
            mid = (glu * jax.nn.sigmoid(SWIGLU_ALPHA * glu) * (lin + 1.0)).astype(BF16)
            part = _dot(mid, wd_bf[lo:hi, :])
            acc = part if acc is None else acc + part
        y_ref[...] = (acc + bd_ref[0]).astype(y_ref.dtype)

    @pl.when(jnp.logical_not(live))
    def _():
        y_ref[...] = jnp.zeros(y_ref.shape, y_ref.dtype)


def _experts(xs, blk_expert, n_used, w_up, b_up, w_down, b_down):
    cap = xs.shape[0]
    n_blocks = cap // EXPERT_ROWS
    live = lambda i, be, nu: jnp.minimum(i, nu[0] - 1)
    return pl.pallas_call(
        _expert_body,
        grid_spec=pltpu.PrefetchScalarGridSpec(
            num_scalar_prefetch=2,
            grid=(n_blocks,),
            in_specs=[pl.BlockSpec((EXPERT_ROWS, D_MODEL), lambda i, be, nu: (live(i, be, nu), 0)),
                      pl.BlockSpec((1, D_MODEL, 2 * D_FF), lambda i, be, nu: (be[live(i, be, nu)], 0, 0)),
                      pl.BlockSpec((1, 1, 2 * D_FF), lambda i, be, nu: (be[live(i, be, nu)], 0, 0)),
                      pl.BlockSpec((1, D_FF, D_MODEL), lambda i, be, nu: (be[live(i, be, nu)], 0, 0)),
                      pl.BlockSpec((1, 1, D_MODEL), lambda i, be, nu: (be[live(i, be, nu)], 0, 0))],
            out_specs=pl.BlockSpec((EXPERT_ROWS, D_MODEL), lambda i, be, nu: (i, 0)),
            scratch_shapes=[pltpu.VMEM((D_MODEL, 2 * D_FF), BF16), pltpu.VMEM((D_FF, D_MODEL), BF16)]),
        out_shape=jax.ShapeDtypeStruct((cap, D_MODEL), BF16),
        compiler_params=_params(("arbitrary",), EXPERT_VMEM_LIMIT),
        name="experts",
    )(blk_expert, n_used, xs, w_up, b_up, w_down, b_down)


def _combine_body(n_p, x1_ref, yg_ref, rt_ref, g_ref, yp_ref, ys_ref):
    i = pl.program_id(0)
    y = x1_ref[...]
    rt = rt_ref[...]
    for k in range(TOP_K):
        y = y + yg_ref[k] * rt[:, TOP_K + k:TOP_K + k + 1]
    ms = jnp.mean(y * y, axis=-1, keepdims=True)
    out = y * lax.rsqrt(ms + NORM_EPS) * g_ref[...]

    @pl.when(i < n_p)
    def _():
        yp_ref[...] = out

    @pl.when(i >= n_p)
    def _():
        ys_ref[...] = out


def _combine(x1, yg, rt, g, t_p, t_s):
    tm = min(512, t_s)
    assert t_p % tm == 0 and t_s % tm == 0
    n_p, n_s = t_p // tm, t_s // tm
    row = lambda i: (i, 0)
    return pl.pallas_call(
        functools.partial(_combine_body, n_p),
        grid=(n_p + n_s,),
        in_specs=[pl.BlockSpec((tm, D_MODEL), row), pl.BlockSpec((TOP_K, tm, D_MODEL), lambda i: (0, i, 0)),
                  pl.BlockSpec((tm, LANES), row), pl.BlockSpec((1, D_MODEL), lambda i: (0, 0))],
        out_specs=[pl.BlockSpec((tm, D_MODEL), lambda i: (jnp.minimum(i, n_p - 1), 0)),
                   pl.BlockSpec((tm, D_MODEL), lambda i: (jnp.maximum(i - n_p, 0), 0))],
        out_shape=[jax.ShapeDtypeStruct((t_p, D_MODEL), F32), jax.ShapeDtypeStruct((t_s, D_MODEL), F32)],
        compiler_params=_params(("arbitrary",)),
        name="combine",
    )(x1, yg, rt, g)


def _route(top_idx):
    t = top_idx.shape[0]
    n = t * TOP_K
    bm = EXPERT_ROWS
    e_flat = top_idx.reshape(-1)
    e_sorted, order = lax.sort((e_flat, jnp.arange(n, dtype=jnp.int32)), num_keys=1, is_stable=True)
    experts = jnp.arange(N_EXPERTS, dtype=jnp.int32)
    counts = jnp.sum((e_flat[:, None] == experts[None, :]).astype(jnp.int32), axis=0)
    padded = ((counts + bm - 1) // bm) * bm
    pad_end = jnp.cumsum(padded)
    pad_start = pad_end - padded
    grp_start = jnp.cumsum(counts) - counts
    n_blocks = -(-n // bm) + N_EXPERTS
    blk_start = jnp.arange(n_blocks, dtype=jnp.int32) * bm
    blk_expert = jnp.minimum(jnp.sum((pad_end[None, :] <= blk_start[:, None]).astype(jnp.int32), axis=1),
                             N_EXPERTS - 1)
    n_used = (pad_end[-1] // bm).astype(jnp.int32).reshape(1)
    shift = (pad_start - grp_start).astype(jnp.int32)
    row_shift = jnp.broadcast_to(shift[blk_expert][:, None], (n_blocks, bm)).reshape(-1)
    row_pos = jnp.clip(jnp.arange(n_blocks * bm, dtype=jnp.int32) - row_shift, 0, n - 1)
    row_tok = order[row_pos] // TOP_K
    shift_sorted = jnp.sum(jnp.where(e_sorted[:, None] == experts[None, :], shift[None, :], 0), axis=1)
    dest_sorted = jnp.arange(n, dtype=jnp.int32) + shift_sorted
    _, dest = lax.sort((order, dest_sorted), num_keys=1)
    return row_tok, blk_expert, n_used, dest.reshape(t, TOP_K).T.reshape(-1)


def kernel(x_prompt, x_sample, cache_k, cache_v, page_table, state_conv, state_C, state_n, state_m,
           norm_mix, w_in, conv_w, conv_b, gate_b, lstm_norm, w_out, norm_ffn, w_router, b_router,
           w_up, b_up, w_down, b_down, norm_final):
    assert w_in.shape[0] == 1, "single-layer stack"
    batch, seq, _ = x_prompt.shape
    db, ds, _ = x_sample.shape
    t_p, t_s = batch * seq, db * ds
    n_pool = cache_k.shape[1]
    assert seq % LSTM_CHUNK == 0 and page_table.shape[1] * PAGE_SIZE % MOBA_BLOCK == 0

    xp = x_prompt.reshape(t_p, D_MODEL)
    xs = x_sample.reshape(t_s, D_MODEL)
    w_main = w_in[0, :, :MAIN_COLS].astype(BF16)
    w_gate = jnp.pad(w_in[0, :, MAIN_COLS:], ((0, 0), (0, LANES - 2 * LSTM_HEADS))).astype(BF16)
    q_a, k_a, v_a, qk_src, v_l, o_l, gts, k_pages, v_pages = _front(xp, xs, norm_mix, w_main, w_gate)

    att_p = _moba(q_a, k_a, v_a, batch, seq)
    k_s, v_s = k_a[t_p:], v_a[t_p:]

    def key_minor(cache):
        cache = cache.reshape(n_pool, ATT_HEADS, PAGE_SIZE, ATT_HEAD_DIM).transpose(0, 1, 3, 2)
        return cache.reshape(n_pool, ATT_WIDTH, PAGE_SIZE)

    att_s = _samp_attn(q_a[t_p:], k_s, v_s, key_minor(cache_k), key_minor(cache_v), page_table, ds).astype(BF16)

    gate_b_row = jnp.pad(gate_b, ((0, 0), (0, LANES - 2 * LSTM_HEADS)))
    cw, cb = conv_w[0], conv_b
    hn_p, conv_p, c_p, n_p_, m_p = _mlstm(
        qk_src, v_l, o_l, gts,
        jnp.zeros((batch * SUBLANES, 2 * LSTM_WIDTH), F32),
        jnp.zeros((batch, LSTM_HEADS, LSTM_HEAD_DIM, LSTM_HEAD_DIM), F32),
        jnp.zeros((batch, SUBLANES, LANES), F32), jnp.zeros((batch, SUBLANES, LANES), F32),
        cw, cb, gate_b_row, lstm_norm, batch, seq // LSTM_CHUNK, LSTM_CHUNK)

    def pad_chunk(a):
        a = a[t_p:].reshape(db, ds, a.shape[-1])
        return jnp.pad(a, ((0, 0), (0, LSTM_CHUNK - ds), (0, 0))).reshape(db * LSTM_CHUNK, a.shape[-1])

    conv0_s = jnp.pad(state_conv[0], ((0, 0), (SUBLANES - (CONV_WIDTH - 1), 0), (0, 0)))
    n0_s = jnp.pad(state_n[0], ((0, 0), (0, SUBLANES - LSTM_HEADS), (0, 0)))
    m0_s = jnp.broadcast_to(jnp.pad(state_m[0], ((0, 0), (0, SUBLANES - LSTM_HEADS)))[:, :, None],
                            (db, SUBLANES, LANES))
    hn_s, conv_s, c_s, n_s, m_s = _mlstm(
        pad_chunk(qk_src), pad_chunk(v_l), pad_chunk(o_l), pad_chunk(gts),
        conv0_s.reshape(db * SUBLANES, 2 * LSTM_WIDTH), state_C[0], n0_s, m0_s,
        cw, cb, gate_b_row, lstm_norm, db, 1, ds)
    hn_s = hn_s.reshape(db, LSTM_CHUNK, LSTM_WIDTH)[:, :ds].reshape(t_s, LSTM_WIDTH)

    wo = w_out[0].astype(BF16)
    w_r = jnp.pad(w_router[0], ((0, 0), (0, LANES - N_EXPERTS))).astype(BF16)
    b_r = jnp.pad(b_router, ((0, 0), (0, LANES - N_EXPERTS)), constant_values=NEG)
    x1, h2, rt = _back(att_p, att_s, hn_p, hn_s, xp, xs, wo[:ATT_WIDTH], wo[ATT_WIDTH:], norm_ffn, w_r, b_r)

    row_tok, blk_expert, n_used, dest = _route(rt[:, :TOP_K].astype(jnp.int32))
    ys = _experts(h2[row_tok], blk_expert, n_used, w_up[0], b_up[0][:, None, :], w_down[0], b_down[0][:, None, :])
    yg = ys[dest].reshape(TOP_K, t_p + t_s, D_MODEL)
    y_p, y_s = _combine(x1, yg, rt, norm_final.reshape(1, D_MODEL), t_p, t_s)

    def to_pages(t):
        t = t.reshape(batch, seq // PAGE_SIZE, ATT_HEADS, ATT_HEAD_DIM, PAGE_SIZE)
        return t.transpose(0, 1, 2, 4, 3)[None]

    def to_slots(t):
        return t.reshape(db, ds, ATT_HEADS, ATT_HEAD_DIM).transpose(0, 2, 1, 3)[None]

    return (y_p.reshape(batch, seq, D_MODEL), y_s.reshape(db, ds, D_MODEL),
            to_pages(k_pages), to_pages(v_pages), to_slots(k_s), to_slots(v_s),
            conv_p[None], conv_s[None],
            c_p[None], n_p_[None, :, :LSTM_HEADS, :], m_p[None, :, :LSTM_HEADS, 0],
            c_s[None], n_s[None, :, :LSTM_HEADS, :], m_s[None, :, :LSTM_HEADS, 0])
```

```python
import functools

import jax
import jax.numpy as jnp
from jax import lax
from jax.experimental import pallas as pl
from jax.experimental.pallas import tpu as pltpu

F32 = jnp.float32
BF16 = jnp.bfloat16

D_MODEL = 1024
ATT_HEADS = 8
ATT_HEAD_DIM = 64
ATT_WIDTH = ATT_HEADS * ATT_HEAD_DIM
MOBA_BLOCK = 256
MOBA_TOPK = 3
LSTM_HEADS = 4
LSTM_HEAD_DIM = 128
LSTM_WIDTH = LSTM_HEADS * LSTM_HEAD_DIM
CONV_WIDTH = 4
LSTM_CHUNK = 128
PAGE_SIZE = 128
N_EXPERTS = 32
TOP_K = 4
D_FF = D_MODEL
SWIGLU_LIMIT = 7.0
SWIGLU_ALPHA = 1.702
NORM_EPS = 1e-5
MAIN_COLS = 3 * ATT_WIDTH + 4 * LSTM_WIDTH
COL_CHUNK = 512
LANES = 128
SUBLANES = 8
EXPERT_ROWS = 512
NEG = -1e30
LOG2E = 1.4426950408889634
N_FEAT = 4
SCORE_AHEAD = 4
VMEM_LIMIT = 48 * 1024 * 1024
EXPERT_VMEM_LIMIT = 56 * 1024 * 1024

_NT = (((1,), (1,)), ((), ()))
_HI = lax.Precision.HIGHEST


def _dot(a, b):
    return jnp.dot(a, b, preferred_element_type=F32)


def _dot_nt(a, b, precision=None):
    return lax.dot_general(a, b, _NT, precision=precision, preferred_element_type=F32)


def _params(sem, vmem_limit=VMEM_LIMIT):
    return pltpu.CompilerParams(dimension_semantics=sem, vmem_limit_bytes=vmem_limit)


def _front_body(n_p, xp_ref, xs_ref, g_ref, w_ref, wg_ref,
                qa_ref, ka_ref, va_ref, qk_ref, vl_ref, ol_ref, gt_ref, kpg_ref, vpg_ref):
    del n_p
    x = jnp.where(pl.program_id(0) == 0, xs_ref[...], xp_ref[...])
    ms = jnp.mean(x * x, axis=-1, keepdims=True)
    h = (x * lax.rsqrt(ms + NORM_EPS) * g_ref[...]).astype(BF16)
    dests = ((qa_ref, 0, None), (ka_ref, 0, kpg_ref), (va_ref, 0, vpg_ref), (qk_ref, 0, None), (qk_ref, 1, None),
             (vl_ref, 0, None), (ol_ref, 0, None))
    for c, (ref, part, pages_ref) in enumerate(dests):
        z = _dot(h, w_ref[:, c * COL_CHUNK:(c + 1) * COL_CHUNK])
        ref[:, part * COL_CHUNK:(part + 1) * COL_CHUNK] = z.astype(ref.dtype)
        if pages_ref is not None:
            for pg in range(pages_ref.shape[0]):
                pages_ref[pg] = z[pg * PAGE_SIZE:(pg + 1) * PAGE_SIZE, :].T
    gt_ref[...] = _dot(h, wg_ref[...])


def _front(xp, xs, g, w_main, w_gate):
    t_p, t_s = xp.shape[0], xs.shape[0]
    tm = t_s
    assert t_p % tm == 0 and tm % 128 == 0
    n_p = t_p // tm
    t_all = t_p + t_s
    prompt = lambda i: jnp.maximum(i - 1, 0)
    row = lambda i: (jnp.where(i == 0, n_p, i - 1), 0)
    const = lambda i: (0, 0)
    wide = lambda n, dt: jax.ShapeDtypeStruct((t_all, n), dt)
    ppt = tm // PAGE_SIZE
    pages = pl.BlockSpec((ppt, ATT_WIDTH, PAGE_SIZE), lambda i: (prompt(i), 0, 0))
    pages_shape = jax.ShapeDtypeStruct((t_p // PAGE_SIZE, ATT_WIDTH, PAGE_SIZE), F32)
    return pl.pallas_call(
        functools.partial(_front_body, n_p),
        grid=(n_p + 1,),
        in_specs=[pl.BlockSpec((tm, D_MODEL), lambda i: (prompt(i), 0)),
                  pl.BlockSpec((tm, D_MODEL), const),
                  pl.BlockSpec((1, D_MODEL), const),
                  pl.BlockSpec((D_MODEL, MAIN_COLS), const),
                  pl.BlockSpec((D_MODEL, LANES), const)],
        out_specs=[pl.BlockSpec((tm, ATT_WIDTH), row), pl.BlockSpec((tm, ATT_WIDTH), row),
                   pl.BlockSpec((tm, ATT_WIDTH), row), pl.BlockSpec((tm, 2 * LSTM_WIDTH), row),
                   pl.BlockSpec((tm, LSTM_WIDTH), row), pl.BlockSpec((tm, LSTM_WIDTH), row),
                   pl.BlockSpec((tm, LANES), row), pages, pages],
        out_shape=[wide(ATT_WIDTH, BF16), wide(ATT_WIDTH, F32), wide(ATT_WIDTH, F32),
                   wide(2 * LSTM_WIDTH, F32), wide(LSTM_WIDTH, F32), wide(LSTM_WIDTH, F32),
                   wide(LANES, F32), pages_shape, pages_shape],
        compiler_params=_params(("arbitrary",)),
        name="front",
    )(xp, xs, g, w_main, w_gate)


def _topk_rows(g, n_valid, k_sel):
    n = g.shape[0]
    jj = lax.broadcasted_iota(jnp.int32, g.shape, 0)
    rank = jnp.zeros(g.shape, F32)
    for j2 in range(n):
        row = g[j2:j2 + 1, :]
        beats = jnp.where(row > g, 1.0, jnp.where((row == g) & (jj > j2), 1.0, 0.0))
        rank = rank + jnp.where(j2 < n_valid, beats, 0.0)
    return jnp.where((jj < n_valid) & (rank < k_sel), 1.0, 0.0)


def _topk_lanes(g, n, k_sel):
    jj = lax.broadcasted_iota(jnp.int32, g.shape, 1)
    rank = jnp.zeros(g.shape, F32)
    for j2 in range(n):
        col = g[:, j2:j2 + 1]
        rank = rank + jnp.where(col > g, 1.0, jnp.where((col == g) & (jj > j2), 1.0, 0.0))
    return jnp.where((jj < n) & (rank < k_sel), 1.0, 0.0)


def _moba_body(nb, q_ref, k_ref, v_ref, o_ref, kaug, vt, kmean):
    i = pl.program_id(1)
    blk = MOBA_BLOCK
    hd = ATT_HEAD_DIM
    seq = nb * blk
    k_sel = min(MOBA_TOPK, nb - 1)
    heads_per = LANES // hd

    @pl.when(i == 0)
    def _():
        row = lax.broadcasted_iota(jnp.int32, (seq, LANES), 0)
        lane_s = lax.broadcasted_iota(jnp.int32, (seq, LANES), 1)
        for c in range(ATT_WIDTH // LANES):
            kf = k_ref[:, c * LANES:(c + 1) * LANES]
            kmean[c] = jnp.sum(kf.reshape(nb, blk, LANES), axis=1) * (1.0 / blk)
            for hl in range(heads_per):
                f = lane_s - (heads_per - 1 - hl) * hd
                feat = jnp.where(f == 0, (row % blk).astype(F32),
                                 jnp.where(f == 1, (row // blk * blk).astype(F32),
                                           jnp.where((f == 2) | (f == 3), 1.0,
                                                     jnp.where(f - N_FEAT == row // blk, 1.0, 0.0))))
                kaug[c * heads_per + hl] = jnp.where((lane_s // hd) == hl, kf, feat).astype(BF16)
        for jb in range(nb):
            vt[jb] = v_ref[jb * blk:(jb + 1) * blk, :].T.astype(BF16)

    key_io = lax.broadcasted_iota(jnp.int32, (blk, blk), 0)
    qry_io = lax.broadcasted_iota(jnp.int32, (blk, blk), 1)
    causal = key_io <= qry_io
    q_row = lax.broadcasted_iota(jnp.int32, (blk, LANES), 0)
    lane = lax.broadcasted_iota(jnp.int32, (blk, LANES), 1)
    blk_io = lax.broadcasted_iota(jnp.int32, (nb, blk), 0)
    own = pl.multiple_of(i * blk, blk)

    q_augs = []
    for hp in range(ATT_WIDTH // LANES):
        sl = slice(hp * LANES, (hp + 1) * LANES)
        q2 = q_ref[:, sl].astype(F32) * (hd ** -0.5)
        km = kmean[hp]
        for hl in range(heads_per):
            head = hp * heads_per + hl
            slope = float(2.0 ** (-8.0 * (head + 1) / ATT_HEADS))
            inhead = (lane // hd) == hl
            gate_t = _dot_nt(km, jnp.where(inhead, q2, 0.0), _HI)
            keep = (_topk_rows(gate_t, i, k_sel) > 0.5) | (blk_io == i)
            spare = (heads_per - 1 - hl) * hd + N_FEAT
            mask_t = jnp.concatenate([jnp.zeros((spare, blk), F32), jnp.where(keep, 0.0, NEG),
                                      jnp.zeros((LANES - spare - nb, blk), F32)], axis=0)
            f = lane - (heads_per - 1 - hl) * hd
            feat = jnp.where(f <= 1, slope,
                             jnp.where(f == 2, -slope * (i * blk).astype(F32),
                                       jnp.where(f == 3, -slope * q_row.astype(F32), mask_t.T)))
            q_augs.append(jnp.where(inhead, q2, feat).astype(BF16))

    def block_pass(j, off, carry):
        def scores(h):
            return _dot_nt(kaug[h, pl.ds(off, blk), :], q_augs[h]) * LOG2E

        ahead = {h: scores(h) for h in range(min(SCORE_AHEAD, ATT_HEADS))}
        out = ()
        for h in range(ATT_HEADS):
            t = ahead.pop(h)
            if carry is None:
                t = jnp.where(causal, t, NEG)
                m_new = jnp.max(t, axis=0, keepdims=True)
            else:
                m, l, acc = carry[3 * h:3 * h + 3]
                m_new = jnp.maximum(m, jnp.max(t, axis=0, keepdims=True))
            p = jnp.exp2(t - m_new)
            if h + SCORE_AHEAD < ATT_HEADS:
                ahead[h + SCORE_AHEAD] = scores(h + SCORE_AHEAD)
            pv = _dot(vt[j, h * hd:(h + 1) * hd, :], p.astype(BF16))
            if carry is None:
                out = out + (m_new, jnp.sum(p, axis=0, keepdims=True), pv)
            else:
                a = jnp.exp2(m - m_new)
                out = out + (m_new, a * l + jnp.sum(p, axis=0, keepdims=True), a * acc + pv)
        return out

    carry = block_pass(i, own, None)
    carry = lax.fori_loop(0, i, lambda j, c: block_pass(j, pl.multiple_of(j * blk, blk), c), carry)
    for hp in range(ATT_WIDTH // LANES):
        o_t = jnp.concatenate([carry[3 * h + 2] / carry[3 * h + 1]
                               for h in range(hp * heads_per, (hp + 1) * heads_per)], axis=0)
        o_ref[:, hp * LANES:(hp + 1) * LANES] = o_t.T.astype(o_ref.dtype)


def _moba(q_all, k_all, v_all, batch, seq):
    nb = seq // MOBA_BLOCK
    assert seq % MOBA_BLOCK == 0 and N_FEAT + nb <= ATT_HEAD_DIM and ATT_HEADS == 8
    return pl.pallas_call(
        functools.partial(_moba_body, nb),
        grid=(batch, nb),
        in_specs=[pl.BlockSpec((MOBA_BLOCK, ATT_WIDTH), lambda b, i: (b * nb + i, 0)),
                  pl.BlockSpec((seq, ATT_WIDTH), lambda b, i: (b, 0)),
                  pl.BlockSpec((seq, ATT_WIDTH), lambda b, i: (b, 0))],
        out_specs=pl.BlockSpec((MOBA_BLOCK, ATT_WIDTH), lambda b, i: (b * nb + i, 0)),
        out_shape=jax.ShapeDtypeStruct((batch * seq, ATT_WIDTH), BF16),
        scratch_shapes=[pltpu.VMEM((ATT_HEADS, seq, LANES), BF16), pltpu.VMEM((nb, ATT_WIDTH, MOBA_BLOCK), BF16),
                        pltpu.VMEM((ATT_WIDTH // LANES, nb, LANES), F32)],
        compiler_params=_params(("arbitrary", "arbitrary")),
        name="moba",
    )(q_all, k_all, v_all)


def _samp_attn_body(n_blk, bps, past_len, dec_seq, pt_ref, qbd_ref, q_ref, kn_ref, vn_ref, *rest):
    del pt_ref
    ppb = MOBA_BLOCK // PAGE_SIZE
    k_refs, v_refs = rest[:ppb * bps], rest[ppb * bps:2 * ppb * bps]
    o_ref, m_scr, l_scr, acc_scr, g_scr = rest[2 * ppb * bps:]
    step = pl.program_id(1)
    hq = ATT_HEADS * SUBLANES
    scale = ATT_HEAD_DIM ** -0.5
    qbd = qbd_ref[0]
    wide = bps * MOBA_BLOCK
    rows = lax.broadcasted_iota(jnp.int32, (hq, wide), 0)
    slope = jnp.exp2((rows // SUBLANES + 1).astype(F32) * (-8.0 / ATT_HEADS))
    tq = (past_len + rows % SUBLANES).astype(F32)
    pos = (step * wide + lax.broadcasted_iota(jnp.int32, (hq, wide), 1)).astype(F32)
    lanes = lax.broadcasted_iota(jnp.int32, (hq, LANES), 1)

    @pl.when(step == 0)
    def _():
        g_scr[...] = jnp.zeros(g_scr.shape, F32)
        m_scr[...] = jnp.zeros(m_scr.shape, F32)
        l_scr[...] = jnp.zeros(l_scr.shape, F32)

    k_cat = jnp.concatenate([r[...].astype(BF16) for r in k_refs], axis=1)
    s_raw = _dot(qbd, k_cat)
    s = s_raw * scale - slope * (tq - pos)
    ones = jnp.ones((MOBA_BLOCK, LANES), BF16)
    g_all, m_all, l_all = g_scr[...], m_scr[...], l_scr[...]
    for bl in range(bps):
        j = step * bps + bl
        seg = slice(bl * MOBA_BLOCK, (bl + 1) * MOBA_BLOCK)
        m = jnp.max(s[:, seg], axis=-1, keepdims=True)
        p = jnp.exp(s[:, seg] - m).astype(BF16)
        l_rep = _dot(p, ones)
        v_cat = jnp.concatenate([v_refs[ppb * bl + pg][...].astype(BF16) for pg in range(ppb)], axis=1)
        acc_scr[j] = _dot_nt(p, v_cat)
        gate = jnp.sum(s_raw[:, seg], axis=-1, keepdims=True) * (1.0 / MOBA_BLOCK)
        hit = lanes == j
        g_all = jnp.where(hit, gate, g_all)
        m_all = jnp.where(hit, m, m_all)
        l_all = jnp.where(hit, l_rep, l_all)
    g_scr[...] = g_all
    m_scr[...] = m_all
    l_scr[...] = l_all

    @pl.when(step == n_blk // bps - 1)
    def _():
        chosen = _topk_lanes(g_all, n_blk, min(MOBA_TOPK, n_blk)) > 0.5

        r2 = lax.broadcasted_iota(jnp.int32, (hq, hq), 0)
        c2 = lax.broadcasted_iota(jnp.int32, (hq, hq), 1)
        ok = ((c2 // SUBLANES) == (r2 // SUBLANES)) & ((c2 % SUBLANES) <= (r2 % SUBLANES)) \
            & ((c2 % SUBLANES) < dec_seq)
        slope2 = jnp.exp2((r2 // SUBLANES + 1).astype(F32) * (-8.0 / ATT_HEADS))
        s = _dot_nt(q_ref[0], kn_ref[0]) * scale - slope2 * (r2 % SUBLANES - c2 % SUBLANES).astype(F32)
        s = jnp.where(ok, s, NEG)
        m_loc = jnp.max(s, axis=-1, keepdims=True)
        p = jnp.exp(s - m_loc)
        l_loc = jnp.sum(p, axis=-1, keepdims=True)
        a_loc = _dot(p.astype(BF16), vn_ref[0])

        m_tot = jnp.maximum(m_loc, jnp.max(jnp.where(chosen, m_all, NEG), axis=-1, keepdims=True))
        w_blk = jnp.where(chosen, jnp.exp(m_all - m_tot), 0.0)
        w_loc = jnp.exp(m_loc - m_tot)
        den = w_loc * l_loc + jnp.sum(w_blk * l_all, axis=-1, keepdims=True)
        num_w = jnp.zeros((hq, ATT_WIDTH), F32)
        for jb in range(n_blk):
            num_w = num_w + w_blk[:, jb:jb + 1] * acc_scr[jb]
        num = jnp.concatenate(
            [num_w[h * SUBLANES:(h + 1) * SUBLANES, h * ATT_HEAD_DIM:(h + 1) * ATT_HEAD_DIM]
             for h in range(ATT_HEADS)], axis=0)
        o_ref[0] = (num + w_loc * a_loc) / den


def _samp_attn(q_s, k_s, v_s, cache_kt, cache_vt, page_table, dec_seq):
    db, n_pages = page_table.shape
    ppb = MOBA_BLOCK // PAGE_SIZE
    assert ppb == 2 and n_pages % ppb == 0 and dec_seq <= SUBLANES
    n_blk = n_pages // ppb
    assert 1 <= n_blk <= LANES
    bps = next(c for c in (8, 4, 2, 1) if n_blk % c == 0)
    past_len = n_pages * PAGE_SIZE
    hq = ATT_HEADS * SUBLANES

    def rows(t):
        t = t.reshape(db, dec_seq, ATT_HEADS, ATT_HEAD_DIM).transpose(0, 2, 1, 3)
        return jnp.pad(t, ((0, 0), (0, 0), (0, SUBLANES - dec_seq), (0, 0)))

    flat = lambda t: t.reshape(db, hq, ATT_HEAD_DIM).astype(BF16)
    q_r = rows(q_s)
    eye = jnp.eye(ATT_HEADS, dtype=q_r.dtype)
    q_bd = (q_r[:, :, :, None, :] * eye[None, :, None, :, None]).reshape(db, hq, ATT_WIDTH).astype(BF16)

    small = pl.BlockSpec((1, hq, ATT_HEAD_DIM), lambda b, s, pt: (b, 0, 0))

    def page(k):
        return pl.BlockSpec((None, ATT_WIDTH, PAGE_SIZE),
                            lambda b, s, pt: (pt[b * n_pages + ppb * bps * s + k], 0, 0))

    pages = [page(k) for k in range(ppb * bps)]
    out = pl.pallas_call(
        functools.partial(_samp_attn_body, n_blk, bps, past_len, dec_seq),
        grid_spec=pltpu.PrefetchScalarGridSpec(
            num_scalar_prefetch=1,
            grid=(db, n_blk // bps),
            in_specs=[pl.BlockSpec((1, hq, ATT_WIDTH), lambda b, s, pt: (b, 0, 0)), small, small, small]
            + pages + pages,
            out_specs=small,
            scratch_shapes=[pltpu.VMEM((hq, LANES), F32), pltpu.VMEM((hq, LANES), F32),
                            pltpu.VMEM((n_blk, hq, ATT_WIDTH), F32), pltpu.VMEM((hq, LANES), F32)]),
        out_shape=jax.ShapeDtypeStruct((db, hq, ATT_HEAD_DIM), F32),
        compiler_params=_params(("arbitrary", "arbitrary")),
        name="samp_attn",
    )(page_table.reshape(-1), q_bd, flat(q_r), flat(rows(k_s)), flat(rows(v_s)),
      *([cache_kt] * (ppb * bps)), *([cache_vt] * (ppb * bps)))
    out = out.reshape(db, ATT_HEADS, SUBLANES, ATT_HEAD_DIM)[:, :, :dec_seq]
    return out.transpose(0, 2, 1, 3).reshape(db * dec_seq, ATT_WIDTH)


def _mlstm_body(l_valid, qk_ref, v_ref, o_ref, g_ref, conv0_ref, c0_ref, n0_ref, m0_ref,
                cw_ref, cb_ref, gb_ref, ln_ref,
                hn_ref, convn_ref, c_ref, n_ref, m_ref, ubuf):
    c = pl.program_id(1)
    lc = LSTM_CHUNK
    hd = LSTM_HEAD_DIM

    @pl.when(c == 0)
    def _():
        ubuf[0:SUBLANES, :] = conv0_ref[...]
        c_ref[...] = c0_ref[...]
        n_ref[...] = n0_ref[...]
        m_ref[...] = m0_ref[...]

    u = qk_ref[...]
    ubuf[SUBLANES:SUBLANES + lc, :] = u
    y = cb_ref[...] + u * cw_ref[CONV_WIDTH - 1:CONV_WIDTH, :]
    for s in range(1, CONV_WIDTH):
        y = y + ubuf[SUBLANES - s:SUBLANES - s + lc, :] * cw_ref[CONV_WIDTH - 1 - s:CONV_WIDTH - s, :]
    qk = y * jax.nn.sigmoid(y)
    convn_ref[0] = ubuf[SUBLANES + l_valid - (CONV_WIDTH - 1):SUBLANES + l_valid, :]
    ubuf[0:SUBLANES, :] = ubuf[lc:lc + SUBLANES, :]

    row = lax.broadcasted_iota(jnp.int32, (lc, lc), 0)
    col = lax.broadcasted_iota(jnp.int32, (lc, lc), 1)
    tri = col <= row
    g = g_ref[...] + gb_ref[...]
    log_f = jnp.minimum(g, 0.0) - jnp.log1p(jnp.exp(-jnp.abs(g)))
    is_f = (col >= LSTM_HEADS) & (col < 2 * LSTM_HEADS)
    gates = jnp.where(is_f, log_f, g)
    if l_valid < lc:
        gates = jnp.where(row < l_valid, gates, jnp.where(is_f, 0.0, NEG))
    csum = jnp.dot(jnp.where(tri, 1.0, 0.0), jnp.where(is_f, gates, 0.0),
                   precision=_HI, preferred_element_type=F32)
    gates_t = gates.T
    csum_t = csum.T

    for h in range(LSTM_HEADS):
        hs = slice(h * hd, (h + 1) * hd)
        qf = qk[:, hs]
        kf = qk[:, LSTM_WIDTH + h * hd:LSTM_WIDTH + (h + 1) * hd] * (hd ** -0.5)
        vf = v_ref[:, hs]
        qb, kb, vb = qf.astype(BF16), kf.astype(BF16), vf.astype(BF16)
        b_col = csum[:, LSTM_HEADS + h:LSTM_HEADS + h + 1]
        b_row = csum_t[LSTM_HEADS + h:LSTM_HEADS + h + 1, :]
        ig_row = gates_t[h:h + 1, :]
        ig_col = gates[:, h:h + 1]
        m0 = m_ref[0, h:h + 1, 0:1]
        n0 = n_ref[0, h:h + 1, :]
        c0 = c_ref[0, h]

        log_d = jnp.where(tri, b_col - b_row + ig_row, NEG)
        dec0 = b_col + m0
        m_t = jnp.maximum(dec0, jnp.max(log_d, axis=-1, keepdims=True))
        d_m = jnp.exp(log_d - m_t)
        w0 = jnp.exp(dec0 - m_t)
        s = _dot_nt(qb, kb) * d_m
        num = w0 * _dot_nt(qb, c0.astype(BF16)) + _dot(s.astype(BF16), vb)
        den = w0 * jnp.sum(qf * n0, axis=-1, keepdims=True) + jnp.sum(s, axis=-1, keepdims=True)
        hh = num / jnp.maximum(jnp.abs(den), jnp.exp(-m_t))

        m_l = m_t[lc - 1:lc, :]
        b_l = b_col[lc - 1:lc, :]
        w_l = jnp.exp(b_l - b_col + ig_col - m_l)
        w0_l = jnp.exp(b_l + m0 - m_l)
        c_ref[0, h] = w0_l * c0 + _dot((w_l * vf).T.astype(BF16), kb)
        n_ref[0, h:h + 1, :] = w0_l * n0 + jnp.sum(w_l * kf, axis=0, keepdims=True)
        m_ref[0, h:h + 1, :] = jnp.broadcast_to(m_l, (1, LANES))

        hn = hh * lax.rsqrt(jnp.mean(hh * hh, axis=-1, keepdims=True) + NORM_EPS)
        hn = hn * ln_ref[:, hs] * jax.nn.sigmoid(o_ref[:, hs])
        hn_ref[:, hs] = hn.astype(hn_ref.dtype)


def _mlstm(qk, v, o, gates, conv0, c0, n0, m0, conv_w, conv_b, gate_b, lstm_norm, batch, n_chunks, l_valid):
    assert CONV_WIDTH - 1 <= l_valid <= LSTM_CHUNK
    nc = n_chunks
    tok = lambda b, c: (b * nc + c, 0)
    per_b3 = lambda b, c: (b, 0, 0)
    const = lambda b, c: (0, 0)
    rows = batch * nc * LSTM_CHUNK
    return pl.pallas_call(
        functools.partial(_mlstm_body, l_valid),
        grid=(batch, nc),
        in_specs=[pl.BlockSpec((LSTM_CHUNK, 2 * LSTM_WIDTH), tok),
                  pl.BlockSpec((LSTM_CHUNK, LSTM_WIDTH), tok),
                  pl.BlockSpec((LSTM_CHUNK, LSTM_WIDTH), tok),
                  pl.BlockSpec((LSTM_CHUNK, LANES), tok),
                  pl.BlockSpec((SUBLANES, 2 * LSTM_WIDTH), lambda b, c: (b, 0)),
                  pl.BlockSpec((1, LSTM_HEADS, LSTM_HEAD_DIM, LSTM_HEAD_DIM), lambda b, c: (b, 0, 0, 0)),
                  pl.BlockSpec((1, SUBLANES, LANES), per_b3),
                  pl.BlockSpec((1, SUBLANES, LANES), per_b3),
                  pl.BlockSpec((CONV_WIDTH, 2 * LSTM_WIDTH), const),
                  pl.BlockSpec((1, 2 * LSTM_WIDTH), const),
                  pl.BlockSpec((1, LANES), const),
                  pl.BlockSpec((1, LSTM_WIDTH), const)],
        out_specs=[pl.BlockSpec((LSTM_CHUNK, LSTM_WIDTH), tok),
                   pl.BlockSpec((1, CONV_WIDTH - 1, 2 * LSTM_WIDTH), per_b3),
                   pl.BlockSpec((1, LSTM_HEADS, LSTM_HEAD_DIM, LSTM_HEAD_DIM), lambda b, c: (b, 0, 0, 0)),
                   pl.BlockSpec((1, SUBLANES, LANES), per_b3),
                   pl.BlockSpec((1, SUBLANES, LANES), per_b3)],
        out_shape=[jax.ShapeDtypeStruct((rows, LSTM_WIDTH), BF16),
                   jax.ShapeDtypeStruct((batch, CONV_WIDTH - 1, 2 * LSTM_WIDTH), F32),
                   jax.ShapeDtypeStruct((batch, LSTM_HEADS, LSTM_HEAD_DIM, LSTM_HEAD_DIM), F32),
                   jax.ShapeDtypeStruct((batch, SUBLANES, LANES), F32),
                   jax.ShapeDtypeStruct((batch, SUBLANES, LANES), F32)],
        scratch_shapes=[pltpu.VMEM((LSTM_CHUNK + 2 * SUBLANES, 2 * LSTM_WIDTH), F32)],
        compiler_params=_params(("arbitrary", "arbitrary")),
        name="mlstm",
    )(qk, v, o, gates, conv0, c0, n0, m0, conv_w, conv_b, gate_b, lstm_norm)


def _back_body(n_p, ap_ref, as_ref, hp_ref, hs_ref, xp_ref, xs_ref, woa_ref, wob_ref, g_ref, wr_ref, br_ref,
               x1_ref, h2_ref, rt_ref, cnt_ref):
    i = pl.program_id(0)
    is_p = i < n_p
    att = jnp.where(is_p, ap_ref[...], as_ref[...])
    hn = jnp.where(is_p, hp_ref[...], hs_ref[...])
    x = jnp.where(is_p, xp_ref[...], xs_ref[...])
    x1 = x + _dot(att, woa_ref[...]) + _dot(hn, wob_ref[...])
    x1_ref[...] = x1
    ms = jnp.mean(x1 * x1, axis=-1, keepdims=True)
    h2 = (x1 * lax.rsqrt(ms + NORM_EPS) * g_ref[...]).astype(BF16)
    h2_ref[...] = h2
    logits = _dot(h2, wr_ref[...]) + br_ref[...]
    lane = lax.broadcasted_iota(jnp.int32, logits.shape, 1).astype(F32)
    cur = logits
    vals, ids = [], []
    for _ in range(TOP_K):
        mk = jnp.max(cur, axis=-1, keepdims=True)
        ik = jnp.min(jnp.where(cur == mk, lane, float(LANES)), axis=-1, keepdims=True)
        cur = jnp.where(lane == ik, -3e38, cur)
        vals.append(mk)
        ids.append(ik)
    es = [jnp.exp(v - vals[0]) for v in vals]
    tot = es[0]
    for e in es[1:]:
        tot = tot + e
    @pl.when(i == 0)
    def _():
        cnt_ref[...] = jnp.zeros(cnt_ref.shape, F32)

    hits = [lane == ik for ik in ids]
    per_expert = jnp.zeros(logits.shape, F32)
    for hit in hits:
        per_expert = jnp.where(hit, 1.0, per_expert)
    tm = logits.shape[0]
    earlier = jnp.where(lax.broadcasted_iota(jnp.int32, (tm, tm), 1) < lax.broadcasted_iota(jnp.int32, (tm, tm), 0),
                        1.0, 0.0).astype(BF16)
    before = _dot(earlier, per_expert.astype(BF16)) + cnt_ref[0:1, :]
    cnt_ref[0:1, :] = cnt_ref[0:1, :] + jnp.sum(per_expert, axis=0, keepdims=True)

    out = jnp.zeros(logits.shape, F32)
    for k in range(TOP_K):
        out = jnp.where(lane == float(k), ids[k], out)
        out = jnp.where(lane == float(TOP_K + k), es[k] / tot, out)
        rank = jnp.sum(jnp.where(hits[k], before, 0.0), axis=-1, keepdims=True)
        out = jnp.where(lane == float(2 * TOP_K + k), rank, out)
    rt_ref[...] = out


def _back(att_p, att_s, hn_p, hn_s, xp, xs, wo_a, wo_b, g, w_r, b_r):
    t_p, t_s = xp.shape[0], xs.shape[0]
    tm = t_s
    n_p = t_p // tm
    t_all = t_p + t_s
    row = lambda i: (i, 0)
    prow = lambda i: (jnp.minimum(i, n_p - 1), 0)
    const = lambda i: (0, 0)
    return pl.pallas_call(
        functools.partial(_back_body, n_p),
        grid=(n_p + 1,),
        in_specs=[pl.BlockSpec((tm, ATT_WIDTH), prow), pl.BlockSpec((tm, ATT_WIDTH), const),
                  pl.BlockSpec((tm, LSTM_WIDTH), prow), pl.BlockSpec((tm, LSTM_WIDTH), const),
                  pl.BlockSpec((tm, D_MODEL), prow), pl.BlockSpec((tm, D_MODEL), const),
                  pl.BlockSpec((ATT_WIDTH, D_MODEL), const), pl.BlockSpec((LSTM_WIDTH, D_MODEL), const),
                  pl.BlockSpec((1, D_MODEL), const), pl.BlockSpec((D_MODEL, LANES), const),
                  pl.BlockSpec((1, LANES), const)],
        out_specs=[pl.BlockSpec((tm, D_MODEL), row), pl.BlockSpec((tm, D_MODEL), row),
                   pl.BlockSpec((tm, LANES), row)],
        out_shape=[jax.ShapeDtypeStruct((t_all, D_MODEL), F32), jax.ShapeDtypeStruct((t_all, D_MODEL), BF16),
                   jax.ShapeDtypeStruct((t_all, LANES), F32)],
        scratch_shapes=[pltpu.VMEM((SUBLANES, LANES), F32)],
        compiler_params=_params(("arbitrary",)),
        name="back",
    )(att_p, att_s, hn_p, hn_s, xp, xs, wo_a, wo_b, g, w_r, b_r)


def _expert_body(be_ref, nu_ref, x_ref, wu_ref, bu_ref, wd_ref, bd_ref, y_ref, wu_bf, wd_bf):
    i = pl.program_id(0)
    live = i < nu_ref[0]

    @pl.when(live & ((i == 0) | (be_ref[i] != be_ref[jnp.maximum(i - 1, 0)])))
    def _():
        wu_bf[...] = wu_ref[0].astype(BF16)
        wd_bf[...] = wd_ref[0].astype(BF16)

    @pl.when(live)
    def _():
        xb = x_ref[...]
        acc = None
        for c in range(D_FF // COL_CHUNK):
            lo, hi = c * COL_CHUNK, (c + 1) * COL_CHUNK
            glu = _dot(xb, wu_bf[:, lo:hi]) + bu_ref[0, :, lo:hi]
            lin = _dot(xb, wu_bf[:, D_FF + lo:D_FF + hi]) + bu_ref[0, :, D_FF + lo:D_FF + hi]
            glu = jnp.minimum(glu, SWIGLU_LIMIT)
            lin = jnp.clip(lin, -SWIGLU_LIMIT, SWIGLU_LIMIT)
            mid = (glu * jax.nn.sigmoid(SWIGLU_ALPHA * glu) * (lin + 1.0)).astype(BF16)
            part = _dot(mid, wd_bf[lo:hi, :])
            acc = part if acc is None else acc + part
        y_ref[...] = (acc + bd_ref[0]).astype(y_ref.dtype)

    @pl.when(jnp.logical_not(live))
    def _():
        y_ref[...] = jnp.zeros(y_ref.shape, y_ref.dtype)


def _experts(xs, blk_expert, n_used, w_up, b_up, w_down, b_down):
    cap = xs.shape[0]
    n_blocks = cap // EXPERT_ROWS
    live = lambda i, be, nu: jnp.minimum(i, nu[0] - 1)
    return pl.pallas_call(
        _expert_body,
        grid_spec=pltpu.PrefetchScalarGridSpec(
            num_scalar_prefetch=2,
            grid=(n_blocks,),
            in_specs=[pl.BlockSpec((EXPERT_ROWS, D_MODEL), lambda i, be, nu: (live(i, be, nu), 0)),
                      pl.BlockSpec((1, D_MODEL, 2 * D_FF), lambda i, be, nu: (be[live(i, be, nu)], 0, 0)),
                      pl.BlockSpec((1, 1, 2 * D_FF), lambda i, be, nu: (be[live(i, be, nu)], 0, 0)),
                      pl.BlockSpec((1, D_FF, D_MODEL), lambda i, be, nu: (be[live(i, be, nu)], 0, 0)),
                      pl.BlockSpec((1, 1, D_MODEL), lambda i, be, nu: (be[live(i, be, nu)], 0, 0))],
            out_specs=pl.BlockSpec((EXPERT_ROWS, D_MODEL), lambda i, be, nu: (i, 0)),
            scratch_shapes=[pltpu.VMEM((D_MODEL, 2 * D_FF), BF16), pltpu.VMEM((D_FF, D_MODEL), BF16)]),
        out_shape=jax.ShapeDtypeStruct((cap, D_MODEL), BF16),
        compiler_params=_params(("arbitrary",), EXPERT_VMEM_LIMIT),
        name="experts",
    )(blk_expert, n_used, xs, w_up, b_up, w_down, b_down)


def _combine_body(n_p, x1_ref, yg_ref, rt_ref, g_ref, yp_ref, ys_ref):
    i = pl.program_id(0)
    y = x1_ref[...]
    rt = rt_ref[...]
    for k in range(TOP_K):
        y = y + yg_ref[k] * rt[:, TOP_K + k:TOP_K + k + 1]
    ms = jnp.mean(y * y, axis=-1, keepdims=True)
    out = y * lax.rsqrt(ms + NORM_EPS) * g_ref[...]

    @pl.when(i < n_p)
    def _():
        yp_ref[...] = out

    @pl.when(i >= n_p)
    def _():
        ys_ref[...] = out


def _combine(x1, yg, rt, g, t_p, t_s):
    tm = min(512, t_s)
    assert t_p % tm == 0 and t_s % tm == 0
    n_p, n_s = t_p // tm, t_s // tm
    row = lambda i: (i, 0)
    return pl.pallas_call(
        functools.partial(_combine_body, n_p),
        grid=(n_p + n_s,),
        in_specs=[pl.BlockSpec((tm, D_MODEL), row), pl.BlockSpec((TOP_K, tm, D_MODEL), lambda i: (0, i, 0)),
                  pl.BlockSpec((tm, LANES), row), pl.BlockSpec((1, D_MODEL), lambda i: (0, 0))],
        out_specs=[pl.BlockSpec((tm, D_MODEL), lambda i: (jnp.minimum(i, n_p - 1), 0)),
                   pl.BlockSpec((tm, D_MODEL), lambda i: (jnp.maximum(i - n_p, 0), 0))],
        out_shape=[jax.ShapeDtypeStruct((t_p, D_MODEL), F32), jax.ShapeDtypeStruct((t_s, D_MODEL), F32)],
        compiler_params=_params(("arbitrary",)),
        name="combine",
    )(x1, yg, rt, g)


def _route(top_idx, rank):
    t = top_idx.shape[0]
    n = t * TOP_K
    bm = EXPERT_ROWS
    idx_bits = (n - 1).bit_length()
    assert N_EXPERTS << idx_bits < 2 ** 31
    e_flat = top_idx.reshape(-1)
    key = lax.sort(e_flat * (1 << idx_bits) + jnp.arange(n, dtype=jnp.int32))
    order = key & ((1 << idx_bits) - 1)
    experts = jnp.arange(N_EXPERTS, dtype=jnp.int32)
    onehot = e_flat[:, None] == experts[None, :]
    counts = jnp.sum(onehot.astype(jnp.int32), axis=0)
    padded = ((counts + bm - 1) // bm) * bm
    pad_end = jnp.cumsum(padded)
    pad_start = pad_end - padded
    grp_start = jnp.cumsum(counts) - counts
    n_blocks = -(-n // bm) + N_EXPERTS
    blk_start = jnp.arange(n_blocks, dtype=jnp.int32) * bm
    blk_expert = jnp.minimum(jnp.sum((pad_end[None, :] <= blk_start[:, None]).astype(jnp.int32), axis=1),
                             N_EXPERTS - 1)
    n_used = (pad_end[-1] // bm).astype(jnp.int32).reshape(1)
    shift = (pad_start - grp_start).astype(jnp.int32)
    row_shift = jnp.broadcast_to(shift[blk_expert][:, None], (n_blocks, bm)).reshape(-1)
    row_pos = jnp.clip(jnp.arange(n_blocks * bm, dtype=jnp.int32) - row_shift, 0, n - 1)
    row_tok = order[row_pos] // TOP_K
    dest = jnp.sum(jnp.where(onehot, pad_start[None, :], 0), axis=1) + rank.reshape(-1)
    return row_tok, blk_expert, n_used, dest.reshape(t, TOP_K).T.reshape(-1)


def kernel(x_prompt, x_sample, cache_k, cache_v, page_table, state_conv, state_C, state_n, state_m,
           norm_mix, w_in, conv_w, conv_b, gate_b, lstm_norm, w_out, norm_ffn, w_router, b_router,
           w_up, b_up, w_down, b_down, norm_final):
    assert w_in.shape[0] == 1, "single-layer stack"
    batch, seq, _ = x_prompt.shape
    db, ds, _ = x_sample.shape
    t_p, t_s = batch * seq, db * ds
    n_pool = cache_k.shape[1]
    assert seq % LSTM_CHUNK == 0 and page_table.shape[1] * PAGE_SIZE % MOBA_BLOCK == 0

    xp = x_prompt.reshape(t_p, D_MODEL)
    xs = x_sample.reshape(t_s, D_MODEL)
    w_main = w_in[0, :, :MAIN_COLS].astype(BF16)
    w_gate = jnp.pad(w_in[0, :, MAIN_COLS:], ((0, 0), (0, LANES - 2 * LSTM_HEADS))).astype(BF16)
    q_a, k_a, v_a, qk_src, v_l, o_l, gts, k_pages, v_pages = _front(xp, xs, norm_mix, w_main, w_gate)

    att_p = _moba(q_a, k_a, v_a, batch, seq)
    k_s, v_s = k_a[t_p:], v_a[t_p:]

    def key_minor(cache):
        cache = cache.reshape(n_pool, ATT_HEADS, PAGE_SIZE, ATT_HEAD_DIM).transpose(0, 1, 3, 2)
        return cache.reshape(n_pool, ATT_WIDTH, PAGE_SIZE)

    att_s = _samp_attn(q_a[t_p:], k_s, v_s, key_minor(cache_k), key_minor(cache_v), page_table, ds).astype(BF16)

    gate_b_row = jnp.pad(gate_b, ((0, 0), (0, LANES - 2 * LSTM_HEADS)))
    cw, cb = conv_w[0], conv_b
    hn_p, conv_p, c_p, n_p_, m_p = _mlstm(
        qk_src, v_l, o_l, gts,
        jnp.zeros((batch * SUBLANES, 2 * LSTM_WIDTH), F32),
        jnp.zeros((batch, LSTM_HEADS, LSTM_HEAD_DIM, LSTM_HEAD_DIM), F32),
        jnp.zeros((batch, SUBLANES, LANES), F32), jnp.zeros((batch, SUBLANES, LANES), F32),
        cw, cb, gate_b_row, lstm_norm, batch, seq // LSTM_CHUNK, LSTM_CHUNK)

    def pad_chunk(a):
        a = a[t_p:].reshape(db, ds, a.shape[-1])
        return jnp.pad(a, ((0, 0), (0, LSTM_CHUNK - ds), (0, 0))).reshape(db * LSTM_CHUNK, a.shape[-1])

    conv0_s = jnp.pad(state_conv[0], ((0, 0), (SUBLANES - (CONV_WIDTH - 1), 0), (0, 0)))
    n0_s = jnp.pad(state_n[0], ((0, 0), (0, SUBLANES - LSTM_HEADS), (0, 0)))
    m0_s = jnp.broadcast_to(jnp.pad(state_m[0], ((0, 0), (0, SUBLANES - LSTM_HEADS)))[:, :, None],
                            (db, SUBLANES, LANES))
    hn_s, conv_s, c_s, n_s, m_s = _mlstm(
        pad_chunk(qk_src), pad_chunk(v_l), pad_chunk(o_l), pad_chunk(gts),
        conv0_s.reshape(db * SUBLANES, 2 * LSTM_WIDTH), state_C[0], n0_s, m0_s,
        cw, cb, gate_b_row, lstm_norm, db, 1, ds)
    hn_s = hn_s.reshape(db, LSTM_CHUNK, LSTM_WIDTH)[:, :ds].reshape(t_s, LSTM_WIDTH)

    wo = w_out[0].astype(BF16)
    w_r = jnp.pad(w_router[0], ((0, 0), (0, LANES - N_EXPERTS))).astype(BF16)
    b_r = jnp.pad(b_router, ((0, 0), (0, LANES - N_EXPERTS)), constant_values=NEG)
    x1, h2, rt = _back(att_p, att_s, hn_p, hn_s, xp, xs, wo[:ATT_WIDTH], wo[ATT_WIDTH:], norm_ffn, w_r, b_r)

    row_tok, blk_expert, n_used, dest = _route(rt[:, :TOP_K].astype(jnp.int32),
                                               rt[:, 2 * TOP_K:3 * TOP_K].astype(jnp.int32))
    ys = _experts(h2[row_tok], blk_expert, n_used, w_up[0], b_up[0][:, None, :], w_down[0], b_down[0][:, None, :])
    yg = ys[dest].reshape(TOP_K, t_p + t_s, D_MODEL)
    y_p, y_s = _combine(x1, yg, rt, norm_final.reshape(1, D_MODEL), t_p, t_s)

    def to_pages(t):
        t = t.reshape(batch, seq // PAGE_SIZE, ATT_HEADS, ATT_HEAD_DIM, PAGE_SIZE)
        return t.transpose(0, 1, 2, 4, 3)[None]

    def to_slots(t):
        return t.reshape(db, ds, ATT_HEADS, ATT_HEAD_DIM).transpose(0, 2, 1, 3)[None]

    return (y_p.reshape(batch, seq, D_MODEL), y_s.reshape(db, ds, D_MODEL),
            to_pages(k_pages), to_pages(v_pages), to_slots(k_s), to_slots(v_s),
            conv_p[None], conv_s[None],
            c_p[None], n_p_[None, :, :LSTM_HEADS, :], m_p[None, :, :LSTM_HEADS, 0],
            c_s[None], n_s[None, :, :LSTM_HEADS, :], m_s[None, :, :LSTM_HEADS, 0])
```

```python
import functools

import jax
import jax.numpy as jnp
from jax import lax
from jax.experimental import pallas as pl
from jax.experimental.pallas import tpu as pltpu

F32 = jnp.float32
BF16 = jnp.bfloat16

D_MODEL = 1024
ATT_HEADS = 8
ATT_HEAD_DIM = 64
ATT_WIDTH = ATT_HEADS * ATT_HEAD_DIM
MOBA_BLOCK = 256
MOBA_TOPK = 3
LSTM_HEADS = 4
LSTM_HEAD_DIM = 128
LSTM_WIDTH = LSTM_HEADS * LSTM_HEAD_DIM
CONV_WIDTH = 4
LSTM_CHUNK = 128
PAGE_SIZE = 128
N_EXPERTS = 32
TOP_K = 4
D_FF = D_MODEL
SWIGLU_LIMIT = 7.0
SWIGLU_ALPHA = 1.702
NORM_EPS = 1e-5
MAIN_COLS = 3 * ATT_WIDTH + 4 * LSTM_WIDTH
COL_CHUNK = 512
LANES = 128
SUBLANES = 8
EXPERT_ROWS = 512
EXPERT_CALLS = 4
NEG = -1e30
LOG2E = 1.4426950408889634
N_FEAT = 4
SCORE_AHEAD = 4
VMEM_LIMIT = 48 * 1024 * 1024
EXPERT_VMEM_LIMIT = 56 * 1024 * 1024

_NT = (((1,), (1,)), ((), ()))
_HI = lax.Precision.HIGHEST


def _dot(a, b):
    return jnp.dot(a, b, preferred_element_type=F32)


def _dot_nt(a, b, precision=None):
    return lax.dot_general(a, b, _NT, precision=precision, preferred_element_type=F32)


def _params(sem, vmem_limit=VMEM_LIMIT):
    return pltpu.CompilerParams(dimension_semantics=sem, vmem_limit_bytes=vmem_limit)


def _front_body(n_p, xp_ref, xs_ref, g_ref, w_ref, wg_ref,
                qa_ref, ka_ref, va_ref, qk_ref, vl_ref, ol_ref, gt_ref, kpg_ref, vpg_ref):
    del n_p
    x = jnp.where(pl.program_id(0) == 0, xs_ref[...], xp_ref[...])
    ms = jnp.mean(x * x, axis=-1, keepdims=True)
    h = (x * lax.rsqrt(ms + NORM_EPS) * g_ref[...]).astype(BF16)
    dests = ((qa_ref, 0, None), (ka_ref, 0, kpg_ref), (va_ref, 0, vpg_ref), (qk_ref, 0, None), (qk_ref, 1, None),
             (vl_ref, 0, None), (ol_ref, 0, None))
    for c, (ref, part, pages_ref) in enumerate(dests):
        z = _dot(h, w_ref[:, c * COL_CHUNK:(c + 1) * COL_CHUNK])
        ref[:, part * COL_CHUNK:(part + 1) * COL_CHUNK] = z.astype(ref.dtype)
        if pages_ref is not None:
            for pg in range(pages_ref.shape[0]):
                pages_ref[pg] = z[pg * PAGE_SIZE:(pg + 1) * PAGE_SIZE, :].T
    gt_ref[...] = _dot(h, wg_ref[...])


def _front(xp, xs, g, w_main, w_gate):
    t_p, t_s = xp.shape[0], xs.shape[0]
    tm = t_s
    assert t_p % tm == 0 and tm % 128 == 0
    n_p = t_p // tm
    t_all = t_p + t_s
    prompt = lambda i: jnp.maximum(i - 1, 0)
    row = lambda i: (jnp.where(i == 0, n_p, i - 1), 0)
    const = lambda i: (0, 0)
    wide = lambda n, dt: jax.ShapeDtypeStruct((t_all, n), dt)
    ppt = tm // PAGE_SIZE
    pages = pl.BlockSpec((ppt, ATT_WIDTH, PAGE_SIZE), lambda i: (prompt(i), 0, 0))
    pages_shape = jax.ShapeDtypeStruct((t_p // PAGE_SIZE, ATT_WIDTH, PAGE_SIZE), F32)
    return pl.pallas_call(
        functools.partial(_front_body, n_p),
        grid=(n_p + 1,),
        in_specs=[pl.BlockSpec((tm, D_MODEL), lambda i: (prompt(i), 0)),
                  pl.BlockSpec((tm, D_MODEL), const),
                  pl.BlockSpec((1, D_MODEL), const),
                  pl.BlockSpec((D_MODEL, MAIN_COLS), const),
                  pl.BlockSpec((D_MODEL, LANES), const)],
        out_specs=[pl.BlockSpec((tm, ATT_WIDTH), row), pl.BlockSpec((tm, ATT_WIDTH), row),
                   pl.BlockSpec((tm, ATT_WIDTH), row), pl.BlockSpec((tm, 2 * LSTM_WIDTH), row),
                   pl.BlockSpec((tm, LSTM_WIDTH), row), pl.BlockSpec((tm, LSTM_WIDTH), row),
                   pl.BlockSpec((tm, LANES), row), pages, pages],
        out_shape=[wide(ATT_WIDTH, BF16), wide(ATT_WIDTH, F32), wide(ATT_WIDTH, F32),
                   wide(2 * LSTM_WIDTH, F32), wide(LSTM_WIDTH, F32), wide(LSTM_WIDTH, F32),
                   wide(LANES, F32), pages_shape, pages_shape],
        compiler_params=_params(("arbitrary",)),
        name="front",
    )(xp, xs, g, w_main, w_gate)


def _topk_rows(g, n_valid, k_sel):
    n = g.shape[0]
    jj = lax.broadcasted_iota(jnp.int32, g.shape, 0)
    rank = jnp.zeros(g.shape, F32)
    for j2 in range(n):
        row = g[j2:j2 + 1, :]
        beats = jnp.where(row > g, 1.0, jnp.where((row == g) & (jj > j2), 1.0, 0.0))
        rank = rank + jnp.where(j2 < n_valid, beats, 0.0)
    return jnp.where((jj < n_valid) & (rank < k_sel), 1.0, 0.0)


def _topk_lanes(g, n, k_sel):
    jj = lax.broadcasted_iota(jnp.int32, g.shape, 1)
    rank = jnp.zeros(g.shape, F32)
    for j2 in range(n):
        col = g[:, j2:j2 + 1]
        rank = rank + jnp.where(col > g, 1.0, jnp.where((col == g) & (jj > j2), 1.0, 0.0))
    return jnp.where((jj < n) & (rank < k_sel), 1.0, 0.0)


def _moba_body(nb, q_ref, k_ref, v_ref, o_ref, kaug, vt, kmean):
    i = pl.program_id(1)
    blk = MOBA_BLOCK
    hd = ATT_HEAD_DIM
    seq = nb * blk
    k_sel = min(MOBA_TOPK, nb - 1)
    heads_per = LANES // hd

    @pl.when(i == 0)
    def _():
        row = lax.broadcasted_iota(jnp.int32, (seq, LANES), 0)
        lane_s = lax.broadcasted_iota(jnp.int32, (seq, LANES), 1)
        for c in range(ATT_WIDTH // LANES):
            kf = k_ref[:, c * LANES:(c + 1) * LANES]
            kmean[c] = jnp.sum(kf.reshape(nb, blk, LANES), axis=1) * (1.0 / blk)
            for hl in range(heads_per):
                f = lane_s - (heads_per - 1 - hl) * hd
                feat = jnp.where(f == 0, (row % blk).astype(F32),
                                 jnp.where(f == 1, (row // blk * blk).astype(F32),
                                           jnp.where((f == 2) | (f == 3), 1.0,
                                                     jnp.where(f - N_FEAT == row // blk, 1.0, 0.0))))
                kaug[c * heads_per + hl] = jnp.where((lane_s // hd) == hl, kf, feat).astype(BF16)
        for jb in range(nb):
            vt[jb] = v_ref[jb * blk:(jb + 1) * blk, :].T.astype(BF16)

    key_io = lax.broadcasted_iota(jnp.int32, (blk, blk), 0)
    qry_io = lax.broadcasted_iota(jnp.int32, (blk, blk), 1)
    causal = key_io <= qry_io
    q_row = lax.broadcasted_iota(jnp.int32, (blk, LANES), 0)
    lane = lax.broadcasted_iota(jnp.int32, (blk, LANES), 1)
    blk_io = lax.broadcasted_iota(jnp.int32, (nb, blk), 0)
    own = pl.multiple_of(i * blk, blk)

    q_augs = []
    for hp in range(ATT_WIDTH // LANES):
        sl = slice(hp * LANES, (hp + 1) * LANES)
        q2 = q_ref[:, sl].astype(F32) * (hd ** -0.5)
        km = kmean[hp]
        for hl in range(heads_per):
            head = hp * heads_per + hl
            slope = float(2.0 ** (-8.0 * (head + 1) / ATT_HEADS))
            inhead = (lane // hd) == hl
            gate_t = _dot_nt(km, jnp.where(inhead, q2, 0.0), _HI)
            keep = (_topk_rows(gate_t, i, k_sel) > 0.5) | (blk_io == i)
            spare = (heads_per - 1 - hl) * hd + N_FEAT
            mask_t = jnp.concatenate([jnp.zeros((spare, blk), F32), jnp.where(keep, 0.0, NEG),
                                      jnp.zeros((LANES - spare - nb, blk), F32)], axis=0)
            f = lane - (heads_per - 1 - hl) * hd
            feat = jnp.where(f <= 1, slope,
                             jnp.where(f == 2, -slope * (i * blk).astype(F32),
                                       jnp.where(f == 3, -slope * q_row.astype(F32), mask_t.T)))
            q_augs.append(jnp.where(inhead, q2, feat).astype(BF16))

    def block_pass(j, off, carry):
        def scores(h):
            return _dot_nt(kaug[h, pl.ds(off, blk), :], q_augs[h]) * LOG2E

        ahead = {h: scores(h) for h in range(min(SCORE_AHEAD, ATT_HEADS))}
        out = ()
        for h in range(ATT_HEADS):
            t = ahead.pop(h)
            if carry is None:
                t = jnp.where(causal, t, NEG)
                m_new = jnp.max(t, axis=0, keepdims=True)
            else:
                m, l, acc = carry[3 * h:3 * h + 3]
                m_new = jnp.maximum(m, jnp.max(t, axis=0, keepdims=True))
            p = jnp.exp2(t - m_new)
            if h + SCORE_AHEAD < ATT_HEADS:
                ahead[h + SCORE_AHEAD] = scores(h + SCORE_AHEAD)
            pv = _dot(vt[j, h * hd:(h + 1) * hd, :], p.astype(BF16))
            if carry is None:
                out = out + (m_new, jnp.sum(p, axis=0, keepdims=True), pv)
            else:
                a = jnp.exp2(m - m_new)
                out = out + (m_new, a * l + jnp.sum(p, axis=0, keepdims=True), a * acc + pv)
        return out

    carry = block_pass(i, own, None)
    carry = lax.fori_loop(0, i, lambda j, c: block_pass(j, pl.multiple_of(j * blk, blk), c), carry)
    for hp in range(ATT_WIDTH // LANES):
        o_t = jnp.concatenate([carry[3 * h + 2] / carry[3 * h + 1]
                               for h in range(hp * heads_per, (hp + 1) * heads_per)], axis=0)
        o_ref[:, hp * LANES:(hp + 1) * LANES] = o_t.T.astype(o_ref.dtype)


def _moba(q_all, k_all, v_all, batch, seq):
    nb = seq // MOBA_BLOCK
    assert seq % MOBA_BLOCK == 0 and N_FEAT + nb <= ATT_HEAD_DIM and ATT_HEADS == 8
    return pl.pallas_call(
        functools.partial(_moba_body, nb),
        grid=(batch, nb),
        in_specs=[pl.BlockSpec((MOBA_BLOCK, ATT_WIDTH), lambda b, i: (b * nb + i, 0)),
                  pl.BlockSpec((seq, ATT_WIDTH), lambda b, i: (b, 0)),
                  pl.BlockSpec((seq, ATT_WIDTH), lambda b, i: (b, 0))],
        out_specs=pl.BlockSpec((MOBA_BLOCK, ATT_WIDTH), lambda b, i: (b * nb + i, 0)),
        out_shape=jax.ShapeDtypeStruct((batch * seq, ATT_WIDTH), BF16),
        scratch_shapes=[pltpu.VMEM((ATT_HEADS, seq, LANES), BF16), pltpu.VMEM((nb, ATT_WIDTH, MOBA_BLOCK), BF16),
                        pltpu.VMEM((ATT_WIDTH // LANES, nb, LANES), F32)],
        compiler_params=_params(("arbitrary", "arbitrary")),
        name="moba",
    )(q_all, k_all, v_all)


def _samp_attn_body(n_blk, bps, past_len, dec_seq, pt_ref, qbd_ref, q_ref, kn_ref, vn_ref, *rest):
    del pt_ref
    ppb = MOBA_BLOCK // PAGE_SIZE
    k_refs, v_refs = rest[:ppb * bps], rest[ppb * bps:2 * ppb * bps]
    o_ref, m_scr, l_scr, acc_scr, g_scr = rest[2 * ppb * bps:]
    step = pl.program_id(1)
    hq = ATT_HEADS * SUBLANES
    scale = ATT_HEAD_DIM ** -0.5
    qbd = qbd_ref[0]
    wide = bps * MOBA_BLOCK
    rows = lax.broadcasted_iota(jnp.int32, (hq, wide), 0)
    slope = jnp.exp2((rows // SUBLANES + 1).astype(F32) * (-8.0 / ATT_HEADS))
    tq = (past_len + rows % SUBLANES).astype(F32)
    pos = (step * wide + lax.broadcasted_iota(jnp.int32, (hq, wide), 1)).astype(F32)
    lanes = lax.broadcasted_iota(jnp.int32, (hq, LANES), 1)

    @pl.when(step == 0)
    def _():
        g_scr[...] = jnp.zeros(g_scr.shape, F32)
        m_scr[...] = jnp.zeros(m_scr.shape, F32)
        l_scr[...] = jnp.zeros(l_scr.shape, F32)

    k_cat = jnp.concatenate([r[...].astype(BF16) for r in k_refs], axis=1)
    s_raw = _dot(qbd, k_cat)
    s = s_raw * scale - slope * (tq - pos)
    ones = jnp.ones((MOBA_BLOCK, LANES), BF16)
    g_all, m_all, l_all = g_scr[...], m_scr[...], l_scr[...]
    for bl in range(bps):
        j = step * bps + bl
        seg = slice(bl * MOBA_BLOCK, (bl + 1) * MOBA_BLOCK)
        m = jnp.max(s[:, seg], axis=-1, keepdims=True)
        p = jnp.exp(s[:, seg] - m).astype(BF16)
        l_rep = _dot(p, ones)
        v_cat = jnp.concatenate([v_refs[ppb * bl + pg][...].astype(BF16) for pg in range(ppb)], axis=1)
        acc_scr[j] = _dot_nt(p, v_cat)
        gate = jnp.sum(s_raw[:, seg], axis=-1, keepdims=True) * (1.0 / MOBA_BLOCK)
        hit = lanes == j
        g_all = jnp.where(hit, gate, g_all)
        m_all = jnp.where(hit, m, m_all)
        l_all = jnp.where(hit, l_rep, l_all)
    g_scr[...] = g_all
    m_scr[...] = m_all
    l_scr[...] = l_all

    @pl.when(step == n_blk // bps - 1)
    def _():
        chosen = _topk_lanes(g_all, n_blk, min(MOBA_TOPK, n_blk)) > 0.5

        r2 = lax.broadcasted_iota(jnp.int32, (hq, hq), 0)
        c2 = lax.broadcasted_iota(jnp.int32, (hq, hq), 1)
        ok = ((c2 // SUBLANES) == (r2 // SUBLANES)) & ((c2 % SUBLANES) <= (r2 % SUBLANES)) \
            & ((c2 % SUBLANES) < dec_seq)
        slope2 = jnp.exp2((r2 // SUBLANES + 1).astype(F32) * (-8.0 / ATT_HEADS))
        s = _dot_nt(q_ref[0], kn_ref[0]) * scale - slope2 * (r2 % SUBLANES - c2 % SUBLANES).astype(F32)
        s = jnp.where(ok, s, NEG)
        m_loc = jnp.max(s, axis=-1, keepdims=True)
        p = jnp.exp(s - m_loc)
        l_loc = jnp.sum(p, axis=-1, keepdims=True)
        a_loc = _dot(p.astype(BF16), vn_ref[0])

        m_tot = jnp.maximum(m_loc, jnp.max(jnp.where(chosen, m_all, NEG), axis=-1, keepdims=True))
        w_blk = jnp.where(chosen, jnp.exp(m_all - m_tot), 0.0)
        w_loc = jnp.exp(m_loc - m_tot)
        den = w_loc * l_loc + jnp.sum(w_blk * l_all, axis=-1, keepdims=True)
        num_w = jnp.zeros((hq, ATT_WIDTH), F32)
        for jb in range(n_blk):
            num_w = num_w + w_blk[:, jb:jb + 1] * acc_scr[jb]
        num = jnp.concatenate(
            [num_w[h * SUBLANES:(h + 1) * SUBLANES, h * ATT_HEAD_DIM:(h + 1) * ATT_HEAD_DIM]
             for h in range(ATT_HEADS)], axis=0)
        o_ref[0] = (num + w_loc * a_loc) / den


def _samp_attn(q_s, k_s, v_s, cache_kt, cache_vt, page_table, dec_seq):
    db, n_pages = page_table.shape
    ppb = MOBA_BLOCK // PAGE_SIZE
    assert ppb == 2 and n_pages % ppb == 0 and dec_seq <= SUBLANES
    n_blk = n_pages // ppb
    assert 1 <= n_blk <= LANES
    bps = next(c for c in (8, 4, 2, 1) if n_blk % c == 0)
    past_len = n_pages * PAGE_SIZE
    hq = ATT_HEADS * SUBLANES

    def rows(t):
        t = t.reshape(db, dec_seq, ATT_HEADS, ATT_HEAD_DIM).transpose(0, 2, 1, 3)
        return jnp.pad(t, ((0, 0), (0, 0), (0, SUBLANES - dec_seq), (0, 0)))

    flat = lambda t: t.reshape(db, hq, ATT_HEAD_DIM).astype(BF16)
    q_r = rows(q_s)
    eye = jnp.eye(ATT_HEADS, dtype=q_r.dtype)
    q_bd = (q_r[:, :, :, None, :] * eye[None, :, None, :, None]).reshape(db, hq, ATT_WIDTH).astype(BF16)

    small = pl.BlockSpec((1, hq, ATT_HEAD_DIM), lambda b, s, pt: (b, 0, 0))

    def page(k):
        return pl.BlockSpec((None, ATT_WIDTH, PAGE_SIZE),
                            lambda b, s, pt: (pt[b * n_pages + ppb * bps * s + k], 0, 0))

    pages = [page(k) for k in range(ppb * bps)]
    out = pl.pallas_call(
        functools.partial(_samp_attn_body, n_blk, bps, past_len, dec_seq),
        grid_spec=pltpu.PrefetchScalarGridSpec(
            num_scalar_prefetch=1,
            grid=(db, n_blk // bps),
            in_specs=[pl.BlockSpec((1, hq, ATT_WIDTH), lambda b, s, pt: (b, 0, 0)), small, small, small]
            + pages + pages,
            out_specs=small,
            scratch_shapes=[pltpu.VMEM((hq, LANES), F32), pltpu.VMEM((hq, LANES), F32),
                            pltpu.VMEM((n_blk, hq, ATT_WIDTH), F32), pltpu.VMEM((hq, LANES), F32)]),
        out_shape=jax.ShapeDtypeStruct((db, hq, ATT_HEAD_DIM), F32),
        compiler_params=_params(("arbitrary", "arbitrary")),
        name="samp_attn",
    )(page_table.reshape(-1), q_bd, flat(q_r), flat(rows(k_s)), flat(rows(v_s)),
      *([cache_kt] * (ppb * bps)), *([cache_vt] * (ppb * bps)))
    out = out.reshape(db, ATT_HEADS, SUBLANES, ATT_HEAD_DIM)[:, :, :dec_seq]
    return out.transpose(0, 2, 1, 3).reshape(db * dec_seq, ATT_WIDTH)


def _mlstm_body(l_valid, qk_ref, v_ref, o_ref, g_ref, conv0_ref, c0_ref, n0_ref, m0_ref,
                cw_ref, cb_ref, gb_ref, ln_ref,
                hn_ref, convn_ref, c_ref, n_ref, m_ref, ubuf):
    c = pl.program_id(1)
    lc = LSTM_CHUNK
    hd = LSTM_HEAD_DIM

    @pl.when(c == 0)
    def _():
        ubuf[0:SUBLANES, :] = conv0_ref[...]
        c_ref[...] = c0_ref[...]
        n_ref[...] = n0_ref[...]
        m_ref[...] = m0_ref[...]

    u = qk_ref[...]
    ubuf[SUBLANES:SUBLANES + lc, :] = u
    y = cb_ref[...] + u * cw_ref[CONV_WIDTH - 1:CONV_WIDTH, :]
    for s in range(1, CONV_WIDTH):
        y = y + ubuf[SUBLANES - s:SUBLANES - s + lc, :] * cw_ref[CONV_WIDTH - 1 - s:CONV_WIDTH - s, :]
    qk = y * jax.nn.sigmoid(y)
    convn_ref[0] = ubuf[SUBLANES + l_valid - (CONV_WIDTH - 1):SUBLANES + l_valid, :]
    ubuf[0:SUBLANES, :] = ubuf[lc:lc + SUBLANES, :]

    row = lax.broadcasted_iota(jnp.int32, (lc, lc), 0)
    col = lax.broadcasted_iota(jnp.int32, (lc, lc), 1)
    tri = col <= row
    g = g_ref[...] + gb_ref[...]
    log_f = jnp.minimum(g, 0.0) - jnp.log1p(jnp.exp(-jnp.abs(g)))
    is_f = (col >= LSTM_HEADS) & (col < 2 * LSTM_HEADS)
    gates = jnp.where(is_f, log_f, g)
    if l_valid < lc:
        gates = jnp.where(row < l_valid, gates, jnp.where(is_f, 0.0, NEG))
    csum = jnp.dot(jnp.where(tri, 1.0, 0.0), jnp.where(is_f, gates, 0.0),
                   precision=_HI, preferred_element_type=F32)
    gates_t = gates.T
    csum_t = csum.T

    for h in range(LSTM_HEADS):
        hs = slice(h * hd, (h + 1) * hd)
        qf = qk[:, hs]
        kf = qk[:, LSTM_WIDTH + h * hd:LSTM_WIDTH + (h + 1) * hd] * (hd ** -0.5)
        vf = v_ref[:, hs]
        qb, kb, vb = qf.astype(BF16), kf.astype(BF16), vf.astype(BF16)
        b_col = csum[:, LSTM_HEADS + h:LSTM_HEADS + h + 1]
        b_row = csum_t[LSTM_HEADS + h:LSTM_HEADS + h + 1, :]
        ig_row = gates_t[h:h + 1, :]
        ig_col = gates[:, h:h + 1]
        m0 = m_ref[0, h:h + 1, 0:1]
        n0 = n_ref[0, h:h + 1, :]
        c0 = c_ref[0, h]

        log_d = jnp.where(tri, b_col - b_row + ig_row, NEG)
        dec0 = b_col + m0
        m_t = jnp.maximum(dec0, jnp.max(log_d, axis=-1, keepdims=True))
        d_m = jnp.exp(log_d - m_t)
        w0 = jnp.exp(dec0 - m_t)
        s = _dot_nt(qb, kb) * d_m
        num = w0 * _dot_nt(qb, c0.astype(BF16)) + _dot(s.astype(BF16), vb)
        den = w0 * jnp.sum(qf * n0, axis=-1, keepdims=True) + jnp.sum(s, axis=-1, keepdims=True)
        hh = num / jnp.maximum(jnp.abs(den), jnp.exp(-m_t))

        m_l = m_t[lc - 1:lc, :]
        b_l = b_col[lc - 1:lc, :]
        w_l = jnp.exp(b_l - b_col + ig_col - m_l)
        w0_l = jnp.exp(b_l + m0 - m_l)
        c_ref[0, h] = w0_l * c0 + _dot((w_l * vf).T.astype(BF16), kb)
        n_ref[0, h:h + 1, :] = w0_l * n0 + jnp.sum(w_l * kf, axis=0, keepdims=True)
        m_ref[0, h:h + 1, :] = jnp.broadcast_to(m_l, (1, LANES))

        hn = hh * lax.rsqrt(jnp.mean(hh * hh, axis=-1, keepdims=True) + NORM_EPS)
        hn = hn * ln_ref[:, hs] * jax.nn.sigmoid(o_ref[:, hs])
        hn_ref[:, hs] = hn.astype(hn_ref.dtype)


def _mlstm(qk, v, o, gates, conv0, c0, n0, m0, conv_w, conv_b, gate_b, lstm_norm, batch, n_chunks, l_valid):
    assert CONV_WIDTH - 1 <= l_valid <= LSTM_CHUNK
    nc = n_chunks
    tok = lambda b, c: (b * nc + c, 0)
    per_b3 = lambda b, c: (b, 0, 0)
    const = lambda b, c: (0, 0)
    rows = batch * nc * LSTM_CHUNK
    return pl.pallas_call(
        functools.partial(_mlstm_body, l_valid),
        grid=(batch, nc),
        in_specs=[pl.BlockSpec((LSTM_CHUNK, 2 * LSTM_WIDTH), tok),
                  pl.BlockSpec((LSTM_CHUNK, LSTM_WIDTH), tok),
                  pl.BlockSpec((LSTM_CHUNK, LSTM_WIDTH), tok),
                  pl.BlockSpec((LSTM_CHUNK, LANES), tok),
                  pl.BlockSpec((SUBLANES, 2 * LSTM_WIDTH), lambda b, c: (b, 0)),
                  pl.BlockSpec((1, LSTM_HEADS, LSTM_HEAD_DIM, LSTM_HEAD_DIM), lambda b, c: (b, 0, 0, 0)),
                  pl.BlockSpec((1, SUBLANES, LANES), per_b3),
                  pl.BlockSpec((1, SUBLANES, LANES), per_b3),
                  pl.BlockSpec((CONV_WIDTH, 2 * LSTM_WIDTH), const),
                  pl.BlockSpec((1, 2 * LSTM_WIDTH), const),
                  pl.BlockSpec((1, LANES), const),
                  pl.BlockSpec((1, LSTM_WIDTH), const)],
        out_specs=[pl.BlockSpec((LSTM_CHUNK, LSTM_WIDTH), tok),
                   pl.BlockSpec((1, CONV_WIDTH - 1, 2 * LSTM_WIDTH), per_b3),
                   pl.BlockSpec((1, LSTM_HEADS, LSTM_HEAD_DIM, LSTM_HEAD_DIM), lambda b, c: (b, 0, 0, 0)),
                   pl.BlockSpec((1, SUBLANES, LANES), per_b3),
                   pl.BlockSpec((1, SUBLANES, LANES), per_b3)],
        out_shape=[jax.ShapeDtypeStruct((rows, LSTM_WIDTH), BF16),
                   jax.ShapeDtypeStruct((batch, CONV_WIDTH - 1, 2 * LSTM_WIDTH), F32),
                   jax.ShapeDtypeStruct((batch, LSTM_HEADS, LSTM_HEAD_DIM, LSTM_HEAD_DIM), F32),
                   jax.ShapeDtypeStruct((batch, SUBLANES, LANES), F32),
                   jax.ShapeDtypeStruct((batch, SUBLANES, LANES), F32)],
        scratch_shapes=[pltpu.VMEM((LSTM_CHUNK + 2 * SUBLANES, 2 * LSTM_WIDTH), F32)],
        compiler_params=_params(("arbitrary", "arbitrary")),
        name="mlstm",
    )(qk, v, o, gates, conv0, c0, n0, m0, conv_w, conv_b, gate_b, lstm_norm)


def _back_body(n_p, ap_ref, as_ref, hp_ref, hs_ref, xp_ref, xs_ref, woa_ref, wob_ref, g_ref, wr_ref, br_ref,
               x1_ref, h2_ref, rt_ref, cnt_ref):
    i = pl.program_id(0)
    is_p = i < n_p
    att = jnp.where(is_p, ap_ref[...], as_ref[...])
    hn = jnp.where(is_p, hp_ref[...], hs_ref[...])
    x = jnp.where(is_p, xp_ref[...], xs_ref[...])
    x1 = x + _dot(att, woa_ref[...]) + _dot(hn, wob_ref[...])
    x1_ref[...] = x1
    ms = jnp.mean(x1 * x1, axis=-1, keepdims=True)
    h2 = (x1 * lax.rsqrt(ms + NORM_EPS) * g_ref[...]).astype(BF16)
    h2_ref[...] = h2
    logits = _dot(h2, wr_ref[...]) + br_ref[...]
    lane = lax.broadcasted_iota(jnp.int32, logits.shape, 1).astype(F32)
    cur = logits
    vals, ids = [], []
    for _ in range(TOP_K):
        mk = jnp.max(cur, axis=-1, keepdims=True)
        ik = jnp.min(jnp.where(cur == mk, lane, float(LANES)), axis=-1, keepdims=True)
        cur = jnp.where(lane == ik, -3e38, cur)
        vals.append(mk)
        ids.append(ik)
    es = [jnp.exp(v - vals[0]) for v in vals]
    tot = es[0]
    for e in es[1:]:
        tot = tot + e
    @pl.when(i == 0)
    def _():
        cnt_ref[...] = jnp.zeros(cnt_ref.shape, F32)

    hits = [lane == ik for ik in ids]
    per_expert = jnp.zeros(logits.shape, F32)
    for hit in hits:
        per_expert = jnp.where(hit, 1.0, per_expert)
    tm = logits.shape[0]
    earlier = jnp.where(lax.broadcasted_iota(jnp.int32, (tm, tm), 1) < lax.broadcasted_iota(jnp.int32, (tm, tm), 0),
                        1.0, 0.0).astype(BF16)
    before = _dot(earlier, per_expert.astype(BF16)) + cnt_ref[0:1, :]
    cnt_ref[0:1, :] = cnt_ref[0:1, :] + jnp.sum(per_expert, axis=0, keepdims=True)

    out = jnp.zeros(logits.shape, F32)
    for k in range(TOP_K):
        out = jnp.where(lane == float(k), ids[k], out)
        out = jnp.where(lane == float(TOP_K + k), es[k] / tot, out)
        rank = jnp.sum(jnp.where(hits[k], before, 0.0), axis=-1, keepdims=True)
        out = jnp.where(lane == float(2 * TOP_K + k), rank, out)
    rt_ref[...] = out


def _back(att_p, att_s, hn_p, hn_s, xp, xs, wo_a, wo_b, g, w_r, b_r):
    t_p, t_s = xp.shape[0], xs.shape[0]
    tm = t_s
    n_p = t_p // tm
    t_all = t_p + t_s
    row = lambda i: (i, 0)
    prow = lambda i: (jnp.minimum(i, n_p - 1), 0)
    const = lambda i: (0, 0)
    return pl.pallas_call(
        functools.partial(_back_body, n_p),
        grid=(n_p + 1,),
        in_specs=[pl.BlockSpec((tm, ATT_WIDTH), prow), pl.BlockSpec((tm, ATT_WIDTH), const),
                  pl.BlockSpec((tm, LSTM_WIDTH), prow), pl.BlockSpec((tm, LSTM_WIDTH), const),
                  pl.BlockSpec((tm, D_MODEL), prow), pl.BlockSpec((tm, D_MODEL), const),
                  pl.BlockSpec((ATT_WIDTH, D_MODEL), const), pl.BlockSpec((LSTM_WIDTH, D_MODEL), const),
                  pl.BlockSpec((1, D_MODEL), const), pl.BlockSpec((D_MODEL, LANES), const),
                  pl.BlockSpec((1, LANES), const)],
        out_specs=[pl.BlockSpec((tm, D_MODEL), row), pl.BlockSpec((tm, D_MODEL), row),
                   pl.BlockSpec((tm, LANES), row)],
        out_shape=[jax.ShapeDtypeStruct((t_all, D_MODEL), F32), jax.ShapeDtypeStruct((t_all, D_MODEL), BF16),
                   jax.ShapeDtypeStruct((t_all, LANES), F32)],
        scratch_shapes=[pltpu.VMEM((SUBLANES, LANES), F32)],
        compiler_params=_params(("arbitrary",)),
        name="back",
    )(att_p, att_s, hn_p, hn_s, xp, xs, wo_a, wo_b, g, w_r, b_r)


def _expert_body(first_block, has_prev, be_ref, nu_ref, x_ref, wu_ref, bu_ref, wd_ref, bd_ref, *rest):
    y_ref, wu_bf, wd_bf = rest[1:] if has_prev else rest
    i = pl.program_id(0)
    g = i + first_block
    live = g < nu_ref[0]

    @pl.when(live & ((i == 0) | (be_ref[g] != be_ref[jnp.maximum(g - 1, 0)])))
    def _():
        wu_bf[...] = wu_ref[0].astype(BF16)
        wd_bf[...] = wd_ref[0].astype(BF16)

    @pl.when(live)
    def _():
        xb = x_ref[...]
        acc = None
        for c in range(D_FF // COL_CHUNK):
            lo, hi = c * COL_CHUNK, (c + 1) * COL_CHUNK
            glu = _dot(xb, wu_bf[:, lo:hi]) + bu_ref[0, :, lo:hi]
            lin = _dot(xb, wu_bf[:, D_FF + lo:D_FF + hi]) + bu_ref[0, :, D_FF + lo:D_FF + hi]
            glu = jnp.minimum(glu, SWIGLU_LIMIT)
            lin = jnp.clip(lin, -SWIGLU_LIMIT, SWIGLU_LIMIT)
            mid = (glu * jax.nn.sigmoid(SWIGLU_ALPHA * glu) * (lin + 1.0)).astype(BF16)
            part = _dot(mid, wd_bf[lo:hi, :])
            acc = part if acc is None else acc + part
        y_ref[...] = (acc + bd_ref[0]).astype(y_ref.dtype)

    @pl.when(jnp.logical_not(live))
    def _():
        y_ref[...] = jnp.zeros(y_ref.shape, y_ref.dtype)


def _experts(x_parts, blk_expert, n_used, w_up, b_up, w_down, b_down):
    n_part = x_parts[0].shape[0] // EXPERT_ROWS
    cap = n_part * EXPERT_ROWS * len(x_parts)
    ys = None
    for p, xs in enumerate(x_parts):
        first = p * n_part
        glob = lambda i, nu, first=first: jnp.minimum(i + first, nu[0] - 1)
        expert = lambda i, be, nu, glob=glob: (be[glob(i, nu)], 0, 0)
        in_specs = [pl.BlockSpec((EXPERT_ROWS, D_MODEL),
                                 lambda i, be, nu, glob=glob, first=first: (jnp.maximum(glob(i, nu) - first, 0), 0)),
                    pl.BlockSpec((1, D_MODEL, 2 * D_FF), expert), pl.BlockSpec((1, 1, 2 * D_FF), expert),
                    pl.BlockSpec((1, D_FF, D_MODEL), expert), pl.BlockSpec((1, 1, D_MODEL), expert)]
        operands = [blk_expert, n_used, xs, w_up, b_up, w_down, b_down]
        if ys is not None:
            in_specs.append(pl.BlockSpec(memory_space=pl.ANY))
            operands.append(ys)
        ys = pl.pallas_call(
            functools.partial(_expert_body, first, p > 0),
            grid_spec=pltpu.PrefetchScalarGridSpec(
                num_scalar_prefetch=2,
                grid=(n_part,),
                in_specs=in_specs,
                out_specs=pl.BlockSpec((EXPERT_ROWS, D_MODEL), lambda i, be, nu, first=first: (i + first, 0)),
                scratch_shapes=[pltpu.VMEM((D_MODEL, 2 * D_FF), BF16), pltpu.VMEM((D_FF, D_MODEL), BF16)]),
            out_shape=jax.ShapeDtypeStruct((cap, D_MODEL), BF16),
            input_output_aliases={len(operands) - 1: 0} if p > 0 else {},
            compiler_params=_params(("arbitrary",), EXPERT_VMEM_LIMIT),
            name="experts",
        )(*operands)
    return ys


def _combine_body(n_p, x1_ref, yg_ref, rt_ref, g_ref, yp_ref, ys_ref):
    i = pl.program_id(0)
    y = x1_ref[...]
    rt = rt_ref[...]
    for k in range(TOP_K):
        y = y + yg_ref[k] * rt[:, TOP_K + k:TOP_K + k + 1]
    ms = jnp.mean(y * y, axis=-1, keepdims=True)
    out = y * lax.rsqrt(ms + NORM_EPS) * g_ref[...]

    @pl.when(i < n_p)
    def _():
        yp_ref[...] = out

    @pl.when(i >= n_p)
    def _():
        ys_ref[...] = out


def _combine(x1, yg, rt, g, t_p, t_s):
    tm = min(512, t_s)
    assert t_p % tm == 0 and t_s % tm == 0
    n_p, n_s = t_p // tm, t_s // tm
    row = lambda i: (i, 0)
    return pl.pallas_call(
        functools.partial(_combine_body, n_p),
        grid=(n_p + n_s,),
        in_specs=[pl.BlockSpec((tm, D_MODEL), row), pl.BlockSpec((TOP_K, tm, D_MODEL), lambda i: (0, i, 0)),
                  pl.BlockSpec((tm, LANES), row), pl.BlockSpec((1, D_MODEL), lambda i: (0, 0))],
        out_specs=[pl.BlockSpec((tm, D_MODEL), lambda i: (jnp.minimum(i, n_p - 1), 0)),
                   pl.BlockSpec((tm, D_MODEL), lambda i: (jnp.maximum(i - n_p, 0), 0))],
        out_shape=[jax.ShapeDtypeStruct((t_p, D_MODEL), F32), jax.ShapeDtypeStruct((t_s, D_MODEL), F32)],
        compiler_params=_params(("arbitrary",)),
        name="combine",
    )(x1, yg, rt, g)


def _route(top_idx, rank):
    t = top_idx.shape[0]
    n = t * TOP_K
    bm = EXPERT_ROWS
    idx_bits = (n - 1).bit_length()
    assert N_EXPERTS << idx_bits < 2 ** 31
    e_flat = top_idx.reshape(-1)
    key = lax.sort(e_flat * (1 << idx_bits) + jnp.arange(n, dtype=jnp.int32))
    order = key & ((1 << idx_bits) - 1)
    experts = jnp.arange(N_EXPERTS, dtype=jnp.int32)
    onehot = e_flat[:, None] == experts[None, :]
    counts = jnp.sum(onehot.astype(jnp.int32), axis=0)
    padded = ((counts + bm - 1) // bm) * bm
    pad_end = jnp.cumsum(padded)
    pad_start = pad_end - padded
    grp_start = jnp.cumsum(counts) - counts
    n_blocks = -(-(-(-n // bm) + N_EXPERTS) // EXPERT_CALLS) * EXPERT_CALLS
    blk_start = jnp.arange(n_blocks, dtype=jnp.int32) * bm
    blk_expert = jnp.minimum(jnp.sum((pad_end[None, :] <= blk_start[:, None]).astype(jnp.int32), axis=1),
                             N_EXPERTS - 1)
    n_used = (pad_end[-1] // bm).astype(jnp.int32).reshape(1)
    shift = (pad_start - grp_start).astype(jnp.int32)
    row_shift = jnp.broadcast_to(shift[blk_expert][:, None], (n_blocks, bm)).reshape(-1)
    row_pos = jnp.clip(jnp.arange(n_blocks * bm, dtype=jnp.int32) - row_shift, 0, n - 1)
    row_tok = order[row_pos] // TOP_K
    dest = jnp.sum(jnp.where(onehot, pad_start[None, :], 0), axis=1) + rank.reshape(-1)
    return row_tok, blk_expert, n_used, dest.reshape(t, TOP_K).T.reshape(-1)


def kernel(x_prompt, x_sample, cache_k, cache_v, page_table, state_conv, state_C, state_n, state_m,
           norm_mix, w_in, conv_w, conv_b, gate_b, lstm_norm, w_out, norm_ffn, w_router, b_router,
           w_up, b_up, w_down, b_down, norm_final):
    assert w_in.shape[0] == 1, "single-layer stack"
    batch, seq, _ = x_prompt.shape
    db, ds, _ = x_sample.shape
    t_p, t_s = batch * seq, db * ds
    n_pool = cache_k.shape[1]
    assert seq % LSTM_CHUNK == 0 and page_table.shape[1] * PAGE_SIZE % MOBA_BLOCK == 0

    xp = x_prompt.reshape(t_p, D_MODEL)
    xs = x_sample.reshape(t_s, D_MODEL)
    w_main = w_in[0, :, :MAIN_COLS].astype(BF16)
    w_gate = jnp.pad(w_in[0, :, MAIN_COLS:], ((0, 0), (0, LANES - 2 * LSTM_HEADS))).astype(BF16)
    q_a, k_a, v_a, qk_src, v_l, o_l, gts, k_pages, v_pages = _front(xp, xs, norm_mix, w_main, w_gate)

    att_p = _moba(q_a, k_a, v_a, batch, seq)
    k_s, v_s = k_a[t_p:], v_a[t_p:]

    def key_minor(cache):
        cache = cache.reshape(n_pool, ATT_HEADS, PAGE_SIZE, ATT_HEAD_DIM).transpose(0, 1, 3, 2)
        return cache.reshape(n_pool, ATT_WIDTH, PAGE_SIZE)

    att_s = _samp_attn(q_a[t_p:], k_s, v_s, key_minor(cache_k), key_minor(cache_v), page_table, ds).astype(BF16)

    gate_b_row = jnp.pad(gate_b, ((0, 0), (0, LANES - 2 * LSTM_HEADS)))
    cw, cb = conv_w[0], conv_b
    hn_p, conv_p, c_p, n_p_, m_p = _mlstm(
        qk_src, v_l, o_l, gts,
        jnp.zeros((batch * SUBLANES, 2 * LSTM_WIDTH), F32),
        jnp.zeros((batch, LSTM_HEADS, LSTM_HEAD_DIM, LSTM_HEAD_DIM), F32),
        jnp.zeros((batch, SUBLANES, LANES), F32), jnp.zeros((batch, SUBLANES, LANES), F32),
        cw, cb, gate_b_row, lstm_norm, batch, seq // LSTM_CHUNK, LSTM_CHUNK)

    def pad_chunk(a):
        a = a[t_p:].reshape(db, ds, a.shape[-1])
        return jnp.pad(a, ((0, 0), (0, LSTM_CHUNK - ds), (0, 0))).reshape(db * LSTM_CHUNK, a.shape[-1])

    conv0_s = jnp.pad(state_conv[0], ((0, 0), (SUBLANES - (CONV_WIDTH - 1), 0), (0, 0)))
    n0_s = jnp.pad(state_n[0], ((0, 0), (0, SUBLANES - LSTM_HEADS), (0, 0)))
    m0_s = jnp.broadcast_to(jnp.pad(state_m[0], ((0, 0), (0, SUBLANES - LSTM_HEADS)))[:, :, None],
                            (db, SUBLANES, LANES))
    hn_s, conv_s, c_s, n_s, m_s = _mlstm(
        pad_chunk(qk_src), pad_chunk(v_l), pad_chunk(o_l), pad_chunk(gts),
        conv0_s.reshape(db * SUBLANES, 2 * LSTM_WIDTH), state_C[0], n0_s, m0_s,
        cw, cb, gate_b_row, lstm_norm, db, 1, ds)
    hn_s = hn_s.reshape(db, LSTM_CHUNK, LSTM_WIDTH)[:, :ds].reshape(t_s, LSTM_WIDTH)

    wo = w_out[0].astype(BF16)
    w_r = jnp.pad(w_router[0], ((0, 0), (0, LANES - N_EXPERTS))).astype(BF16)
    b_r = jnp.pad(b_router, ((0, 0), (0, LANES - N_EXPERTS)), constant_values=NEG)
    x1, h2, rt = _back(att_p, att_s, hn_p, hn_s, xp, xs, wo[:ATT_WIDTH], wo[ATT_WIDTH:], norm_ffn, w_r, b_r)

    row_tok, blk_expert, n_used, dest = _route(rt[:, :TOP_K].astype(jnp.int32),
                                               rt[:, 2 * TOP_K:3 * TOP_K].astype(jnp.int32))
    x_parts = [h2[piece] for piece in jnp.split(row_tok, EXPERT_CALLS)]
    ys = _experts(x_parts, blk_expert, n_used, w_up[0], b_up[0][:, None, :], w_down[0], b_down[0][:, None, :])
    yg = ys[dest].reshape(TOP_K, t_p + t_s, D_MODEL)
    y_p, y_s = _combine(x1, yg, rt, norm_final.reshape(1, D_MODEL), t_p, t_s)

    def to_pages(t):
        t = t.reshape(batch, seq // PAGE_SIZE, ATT_HEADS, ATT_HEAD_DIM, PAGE_SIZE)
        return t.transpose(0, 1, 2, 4, 3)[None]

    def to_slots(t):
        return t.reshape(db, ds, ATT_HEADS, ATT_HEAD_DIM).transpose(0, 2, 1, 3)[None]

    return (y_p.reshape(batch, seq, D_MODEL), y_s.reshape(db, ds, D_MODEL),
            to_pages(k_pages), to_pages(v_pages), to_slots(k_s), to_slots(v_s),
            conv_p[None], conv_s[None],
            c_p[None], n_p_[None, :, :LSTM_HEADS, :], m_p[None, :, :LSTM_HEADS, 0],
            c_s[None], n_s[None, :, :LSTM_HEADS, :], m_s[None, :, :LSTM_HEADS, 0])
```

```python
import functools

import jax
import jax.numpy as jnp
from jax import lax
from jax.experimental import pallas as pl
from jax.experimental.pallas import tpu as pltpu

F32 = jnp.float32
BF16 = jnp.bfloat16

D_MODEL = 1024
ATT_HEADS = 8
ATT_HEAD_DIM = 64
ATT_WIDTH = ATT_HEADS * ATT_HEAD_DIM
MOBA_BLOCK = 256
MOBA_TOPK = 3
LSTM_HEADS = 4
LSTM_HEAD_DIM = 128
LSTM_WIDTH = LSTM_HEADS * LSTM_HEAD_DIM
CONV_WIDTH = 4
LSTM_CHUNK = 128
PAGE_SIZE = 128
N_EXPERTS = 32
TOP_K = 4
D_FF = D_MODEL
SWIGLU_LIMIT = 7.0
SWIGLU_ALPHA = 1.702
NORM_EPS = 1e-5
MAIN_COLS = 3 * ATT_WIDTH + 4 * LSTM_WIDTH
COL_CHUNK = 512
LANES = 128
SUBLANES = 8
EXPERT_ROWS = 512
EXPERT_CALLS = 4
COMBINE_CALLS = 4
NEG = -1e30
LOG2E = 1.4426950408889634
N_FEAT = 4
SCORE_AHEAD = 4
VMEM_LIMIT = 48 * 1024 * 1024
EXPERT_VMEM_LIMIT = 56 * 1024 * 1024

_NT = (((1,), (1,)), ((), ()))
_HI = lax.Precision.HIGHEST


def _dot(a, b):
    return jnp.dot(a, b, preferred_element_type=F32)


def _dot_nt(a, b, precision=None):
    return lax.dot_general(a, b, _NT, precision=precision, preferred_element_type=F32)


def _params(sem, vmem_limit=VMEM_LIMIT):
    return pltpu.CompilerParams(dimension_semantics=sem, vmem_limit_bytes=vmem_limit)


def _front_body(n_p, xp_ref, xs_ref, g_ref, w_ref, wg_ref,
                qa_ref, ka_ref, va_ref, qk_ref, vl_ref, ol_ref, gt_ref, kpg_ref, vpg_ref):
    del n_p
    x = jnp.where(pl.program_id(0) == 0, xs_ref[...], xp_ref[...])
    ms = jnp.mean(x * x, axis=-1, keepdims=True)
    h = (x * lax.rsqrt(ms + NORM_EPS) * g_ref[...]).astype(BF16)
    dests = ((qa_ref, 0, None), (ka_ref, 0, kpg_ref), (va_ref, 0, vpg_ref), (qk_ref, 0, None), (qk_ref, 1, None),
             (vl_ref, 0, None), (ol_ref, 0, None))
    for c, (ref, part, pages_ref) in enumerate(dests):
        z = _dot(h, w_ref[:, c * COL_CHUNK:(c + 1) * COL_CHUNK])
        ref[:, part * COL_CHUNK:(part + 1) * COL_CHUNK] = z.astype(ref.dtype)
        if pages_ref is not None:
            for pg in range(pages_ref.shape[0]):
                pages_ref[pg] = z[pg * PAGE_SIZE:(pg + 1) * PAGE_SIZE, :].T
    gt_ref[...] = _dot(h, wg_ref[...])


def _front(xp, xs, g, w_main, w_gate):
    t_p, t_s = xp.shape[0], xs.shape[0]
    tm = t_s
    assert t_p % tm == 0 and tm % 128 == 0
    n_p = t_p // tm
    t_all = t_p + t_s
    prompt = lambda i: jnp.maximum(i - 1, 0)
    row = lambda i: (jnp.where(i == 0, n_p, i - 1), 0)
    const = lambda i: (0, 0)
    wide = lambda n, dt: jax.ShapeDtypeStruct((t_all, n), dt)
    ppt = tm // PAGE_SIZE
    pages = pl.BlockSpec((ppt, ATT_WIDTH, PAGE_SIZE), lambda i: (prompt(i), 0, 0))
    pages_shape = jax.ShapeDtypeStruct((t_p // PAGE_SIZE, ATT_WIDTH, PAGE_SIZE), F32)
    return pl.pallas_call(
        functools.partial(_front_body, n_p),
        grid=(n_p + 1,),
        in_specs=[pl.BlockSpec((tm, D_MODEL), lambda i: (prompt(i), 0)),
                  pl.BlockSpec((tm, D_MODEL), const),
                  pl.BlockSpec((1, D_MODEL), const),
                  pl.BlockSpec((D_MODEL, MAIN_COLS), const),
                  pl.BlockSpec((D_MODEL, LANES), const)],
        out_specs=[pl.BlockSpec((tm, ATT_WIDTH), row), pl.BlockSpec((tm, ATT_WIDTH), row),
                   pl.BlockSpec((tm, ATT_WIDTH), row), pl.BlockSpec((tm, 2 * LSTM_WIDTH), row),
                   pl.BlockSpec((tm, LSTM_WIDTH), row), pl.BlockSpec((tm, LSTM_WIDTH), row),
                   pl.BlockSpec((tm, LANES), row), pages, pages],
        out_shape=[wide(ATT_WIDTH, BF16), wide(ATT_WIDTH, F32), wide(ATT_WIDTH, F32),
                   wide(2 * LSTM_WIDTH, F32), wide(LSTM_WIDTH, F32), wide(LSTM_WIDTH, F32),
                   wide(LANES, F32), pages_shape, pages_shape],
        compiler_params=_params(("arbitrary",)),
        name="front",
    )(xp, xs, g, w_main, w_gate)


def _topk_rows(g, n_valid, k_sel):
    n = g.shape[0]
    jj = lax.broadcasted_iota(jnp.int32, g.shape, 0)
    rank = jnp.zeros(g.shape, F32)
    for j2 in range(n):
        row = g[j2:j2 + 1, :]
        beats = jnp.where(row > g, 1.0, jnp.where((row == g) & (jj > j2), 1.0, 0.0))
        rank = rank + jnp.where(j2 < n_valid, beats, 0.0)
    return jnp.where((jj < n_valid) & (rank < k_sel), 1.0, 0.0)


def _topk_lanes(g, n, k_sel):
    jj = lax.broadcasted_iota(jnp.int32, g.shape, 1)
    rank = jnp.zeros(g.shape, F32)
    for j2 in range(n):
        col = g[:, j2:j2 + 1]
        rank = rank + jnp.where(col > g, 1.0, jnp.where((col == g) & (jj > j2), 1.0, 0.0))
    return jnp.where((jj < n) & (rank < k_sel), 1.0, 0.0)


def _moba_body(nb, q_ref, k_ref, v_ref, o_ref, kaug, vt, kmean):
    i = pl.program_id(1)
    blk = MOBA_BLOCK
    hd = ATT_HEAD_DIM
    seq = nb * blk
    k_sel = min(MOBA_TOPK, nb - 1)
    heads_per = LANES // hd

    @pl.when(i == 0)
    def _():
        row = lax.broadcasted_iota(jnp.int32, (seq, LANES), 0)
        lane_s = lax.broadcasted_iota(jnp.int32, (seq, LANES), 1)
        for c in range(ATT_WIDTH // LANES):
            kf = k_ref[:, c * LANES:(c + 1) * LANES]
            kmean[c] = jnp.sum(kf.reshape(nb, blk, LANES), axis=1) * (1.0 / blk)
            for hl in range(heads_per):
                f = lane_s - (heads_per - 1 - hl) * hd
                feat = jnp.where(f == 0, (row % blk).astype(F32),
                                 jnp.where(f == 1, (row // blk * blk).astype(F32),
                                           jnp.where((f == 2) | (f == 3), 1.0,
                                                     jnp.where(f - N_FEAT == row // blk, 1.0, 0.0))))
                kaug[c * heads_per + hl] = jnp.where((lane_s // hd) == hl, kf, feat).astype(BF16)
        for jb in range(nb):
            vt[jb] = v_ref[jb * blk:(jb + 1) * blk, :].T.astype(BF16)

    key_io = lax.broadcasted_iota(jnp.int32, (blk, blk), 0)
    qry_io = lax.broadcasted_iota(jnp.int32, (blk, blk), 1)
    causal = key_io <= qry_io
    q_row = lax.broadcasted_iota(jnp.int32, (blk, LANES), 0)
    lane = lax.broadcasted_iota(jnp.int32, (blk, LANES), 1)
    blk_io = lax.broadcasted_iota(jnp.int32, (nb, blk), 0)
    own = pl.multiple_of(i * blk, blk)

    q_augs = []
    for hp in range(ATT_WIDTH // LANES):
        sl = slice(hp * LANES, (hp + 1) * LANES)
        q2 = q_ref[:, sl].astype(F32) * (hd ** -0.5)
        km = kmean[hp]
        for hl in range(heads_per):
            head = hp * heads_per + hl
            slope = float(2.0 ** (-8.0 * (head + 1) / ATT_HEADS))
            inhead = (lane // hd) == hl
            gate_t = _dot_nt(km, jnp.where(inhead, q2, 0.0), _HI)
            keep = (_topk_rows(gate_t, i, k_sel) > 0.5) | (blk_io == i)
            spare = (heads_per - 1 - hl) * hd + N_FEAT
            mask_t = jnp.concatenate([jnp.zeros((spare, blk), F32), jnp.where(keep, 0.0, NEG),
                                      jnp.zeros((LANES - spare - nb, blk), F32)], axis=0)
            f = lane - (heads_per - 1 - hl) * hd
            feat = jnp.where(f <= 1, slope,
                             jnp.where(f == 2, -slope * (i * blk).astype(F32),
                                       jnp.where(f == 3, -slope * q_row.astype(F32), mask_t.T)))
            q_augs.append(jnp.where(inhead, q2, feat).astype(BF16))

    def block_pass(j, off, carry):
        def scores(h):
            return _dot_nt(kaug[h, pl.ds(off, blk), :], q_augs[h]) * LOG2E

        ahead = {h: scores(h) for h in range(min(SCORE_AHEAD, ATT_HEADS))}
        out = ()
        for h in range(ATT_HEADS):
            t = ahead.pop(h)
            if carry is None:
                t = jnp.where(causal, t, NEG)
                m_new = jnp.max(t, axis=0, keepdims=True)
            else:
                m, l, acc = carry[3 * h:3 * h + 3]
                m_new = jnp.maximum(m, jnp.max(t, axis=0, keepdims=True))
            p = jnp.exp2(t - m_new)
            if h + SCORE_AHEAD < ATT_HEADS:
                ahead[h + SCORE_AHEAD] = scores(h + SCORE_AHEAD)
            pv = _dot(vt[j, h * hd:(h + 1) * hd, :], p.astype(BF16))
            if carry is None:
                out = out + (m_new, jnp.sum(p, axis=0, keepdims=True), pv)
            else:
                a = jnp.exp2(m - m_new)
                out = out + (m_new, a * l + jnp.sum(p, axis=0, keepdims=True), a * acc + pv)
        return out

    carry = block_pass(i, own, None)
    carry = lax.fori_loop(0, i, lambda j, c: block_pass(j, pl.multiple_of(j * blk, blk), c), carry)
    for hp in range(ATT_WIDTH // LANES):
        o_t = jnp.concatenate([carry[3 * h + 2] / carry[3 * h + 1]
                               for h in range(hp * heads_per, (hp + 1) * heads_per)], axis=0)
        o_ref[:, hp * LANES:(hp + 1) * LANES] = o_t.T.astype(o_ref.dtype)


def _moba(q_all, k_all, v_all, batch, seq):
    nb = seq // MOBA_BLOCK
    assert seq % MOBA_BLOCK == 0 and N_FEAT + nb <= ATT_HEAD_DIM and ATT_HEADS == 8
    return pl.pallas_call(
        functools.partial(_moba_body, nb),
        grid=(batch, nb),
        in_specs=[pl.BlockSpec((MOBA_BLOCK, ATT_WIDTH), lambda b, i: (b * nb + i, 0)),
                  pl.BlockSpec((seq, ATT_WIDTH), lambda b, i: (b, 0)),
                  pl.BlockSpec((seq, ATT_WIDTH), lambda b, i: (b, 0))],
        out_specs=pl.BlockSpec((MOBA_BLOCK, ATT_WIDTH), lambda b, i: (b * nb + i, 0)),
        out_shape=jax.ShapeDtypeStruct((batch * seq, ATT_WIDTH), BF16),
        scratch_shapes=[pltpu.VMEM((ATT_HEADS, seq, LANES), BF16), pltpu.VMEM((nb, ATT_WIDTH, MOBA_BLOCK), BF16),
                        pltpu.VMEM((ATT_WIDTH // LANES, nb, LANES), F32)],
        compiler_params=_params(("arbitrary", "arbitrary")),
        name="moba",
    )(q_all, k_all, v_all)


def _samp_attn_body(n_blk, bps, past_len, dec_seq, pt_ref, qbd_ref, q_ref, kn_ref, vn_ref, *rest):
    del pt_ref
    ppb = MOBA_BLOCK // PAGE_SIZE
    k_refs, v_refs = rest[:ppb * bps], rest[ppb * bps:2 * ppb * bps]
    o_ref, m_scr, l_scr, acc_scr, g_scr = rest[2 * ppb * bps:]
    step = pl.program_id(1)
    hq = ATT_HEADS * SUBLANES
    scale = ATT_HEAD_DIM ** -0.5
    qbd = qbd_ref[0]
    wide = bps * MOBA_BLOCK
    rows = lax.broadcasted_iota(jnp.int32, (hq, wide), 0)
    slope = jnp.exp2((rows // SUBLANES + 1).astype(F32) * (-8.0 / ATT_HEADS))
    tq = (past_len + rows % SUBLANES).astype(F32)
    pos = (step * wide + lax.broadcasted_iota(jnp.int32, (hq, wide), 1)).astype(F32)
    lanes = lax.broadcasted_iota(jnp.int32, (hq, LANES), 1)

    @pl.when(step == 0)
    def _():
        g_scr[...] = jnp.zeros(g_scr.shape, F32)
        m_scr[...] = jnp.zeros(m_scr.shape, F32)
        l_scr[...] = jnp.zeros(l_scr.shape, F32)

    k_cat = jnp.concatenate([r[...].astype(BF16) for r in k_refs], axis=1)
    s_raw = _dot(qbd, k_cat)
    s = s_raw * scale - slope * (tq - pos)
    ones = jnp.ones((MOBA_BLOCK, LANES), BF16)
    g_all, m_all, l_all = g_scr[...], m_scr[...], l_scr[...]
    for bl in range(bps):
        j = step * bps + bl
        seg = slice(bl * MOBA_BLOCK, (bl + 1) * MOBA_BLOCK)
        m = jnp.max(s[:, seg], axis=-1, keepdims=True)
        p = jnp.exp(s[:, seg] - m).astype(BF16)
        l_rep = _dot(p, ones)
        v_cat = jnp.concatenate([v_refs[ppb * bl + pg][...].astype(BF16) for pg in range(ppb)], axis=1)
        acc_scr[j] = _dot_nt(p, v_cat)
        gate = jnp.sum(s_raw[:, seg], axis=-1, keepdims=True) * (1.0 / MOBA_BLOCK)
        hit = lanes == j
        g_all = jnp.where(hit, gate, g_all)
        m_all = jnp.where(hit, m, m_all)
        l_all = jnp.where(hit, l_rep, l_all)
    g_scr[...] = g_all
    m_scr[...] = m_all
    l_scr[...] = l_all

    @pl.when(step == n_blk // bps - 1)
    def _():
        chosen = _topk_lanes(g_all, n_blk, min(MOBA_TOPK, n_blk)) > 0.5

        r2 = lax.broadcasted_iota(jnp.int32, (hq, hq), 0)
        c2 = lax.broadcasted_iota(jnp.int32, (hq, hq), 1)
        ok = ((c2 // SUBLANES) == (r2 // SUBLANES)) & ((c2 % SUBLANES) <= (r2 % SUBLANES)) \
            & ((c2 % SUBLANES) < dec_seq)
        slope2 = jnp.exp2((r2 // SUBLANES + 1).astype(F32) * (-8.0 / ATT_HEADS))
        s = _dot_nt(q_ref[0], kn_ref[0]) * scale - slope2 * (r2 % SUBLANES - c2 % SUBLANES).astype(F32)
        s = jnp.where(ok, s, NEG)
        m_loc = jnp.max(s, axis=-1, keepdims=True)
        p = jnp.exp(s - m_loc)
        l_loc = jnp.sum(p, axis=-1, keepdims=True)
        a_loc = _dot(p.astype(BF16), vn_ref[0])

        m_tot = jnp.maximum(m_loc, jnp.max(jnp.where(chosen, m_all, NEG), axis=-1, keepdims=True))
        w_blk = jnp.where(chosen, jnp.exp(m_all - m_tot), 0.0)
        w_loc = jnp.exp(m_loc - m_tot)
        den = w_loc * l_loc + jnp.sum(w_blk * l_all, axis=-1, keepdims=True)
        num_w = jnp.zeros((hq, ATT_WIDTH), F32)
        for jb in range(n_blk):
            num_w = num_w + w_blk[:, jb:jb + 1] * acc_scr[jb]
        num = jnp.concatenate(
            [num_w[h * SUBLANES:(h + 1) * SUBLANES, h * ATT_HEAD_DIM:(h + 1) * ATT_HEAD_DIM]
             for h in range(ATT_HEADS)], axis=0)
        o_ref[0] = (num + w_loc * a_loc) / den


def _samp_attn(q_s, k_s, v_s, cache_kt, cache_vt, page_table, dec_seq):
    db, n_pages = page_table.shape
    ppb = MOBA_BLOCK // PAGE_SIZE
    assert ppb == 2 and n_pages % ppb == 0 and dec_seq <= SUBLANES
    n_blk = n_pages // ppb
    assert 1 <= n_blk <= LANES
    bps = next(c for c in (8, 4, 2, 1) if n_blk % c == 0)
    past_len = n_pages * PAGE_SIZE
    hq = ATT_HEADS * SUBLANES

    def rows(t):
        t = t.reshape(db, dec_seq, ATT_HEADS, ATT_HEAD_DIM).transpose(0, 2, 1, 3)
        return jnp.pad(t, ((0, 0), (0, 0), (0, SUBLANES - dec_seq), (0, 0)))

    flat = lambda t: t.reshape(db, hq, ATT_HEAD_DIM).astype(BF16)
    q_r = rows(q_s)
    eye = jnp.eye(ATT_HEADS, dtype=q_r.dtype)
    q_bd = (q_r[:, :, :, None, :] * eye[None, :, None, :, None]).reshape(db, hq, ATT_WIDTH).astype(BF16)

    small = pl.BlockSpec((1, hq, ATT_HEAD_DIM), lambda b, s, pt: (b, 0, 0))

    def page(k):
        return pl.BlockSpec((None, ATT_WIDTH, PAGE_SIZE),
                            lambda b, s, pt: (pt[b * n_pages + ppb * bps * s + k], 0, 0))

    pages = [page(k) for k in range(ppb * bps)]
    out = pl.pallas_call(
        functools.partial(_samp_attn_body, n_blk, bps, past_len, dec_seq),
        grid_spec=pltpu.PrefetchScalarGridSpec(
            num_scalar_prefetch=1,
            grid=(db, n_blk // bps),
            in_specs=[pl.BlockSpec((1, hq, ATT_WIDTH), lambda b, s, pt: (b, 0, 0)), small, small, small]
            + pages + pages,
            out_specs=small,
            scratch_shapes=[pltpu.VMEM((hq, LANES), F32), pltpu.VMEM((hq, LANES), F32),
                            pltpu.VMEM((n_blk, hq, ATT_WIDTH), F32), pltpu.VMEM((hq, LANES), F32)]),
        out_shape=jax.ShapeDtypeStruct((db, hq, ATT_HEAD_DIM), F32),
        compiler_params=_params(("arbitrary", "arbitrary")),
        name="samp_attn",
    )(page_table.reshape(-1), q_bd, flat(q_r), flat(rows(k_s)), flat(rows(v_s)),
      *([cache_kt] * (ppb * bps)), *([cache_vt] * (ppb * bps)))
    out = out.reshape(db, ATT_HEADS, SUBLANES, ATT_HEAD_DIM)[:, :, :dec_seq]
    return out.transpose(0, 2, 1, 3).reshape(db * dec_seq, ATT_WIDTH)


def _mlstm_body(l_valid, qk_ref, v_ref, o_ref, g_ref, conv0_ref, c0_ref, n0_ref, m0_ref,
                cw_ref, cb_ref, gb_ref, ln_ref,
                hn_ref, convn_ref, c_ref, n_ref, m_ref, ubuf):
    c = pl.program_id(1)
    lc = LSTM_CHUNK
    hd = LSTM_HEAD_DIM

    @pl.when(c == 0)
    def _():
        ubuf[0:SUBLANES, :] = conv0_ref[...]
        c_ref[...] = c0_ref[...]
        n_ref[...] = n0_ref[...]
        m_ref[...] = m0_ref[...]

    u = qk_ref[...]
    ubuf[SUBLANES:SUBLANES + lc, :] = u
    y = cb_ref[...] + u * cw_ref[CONV_WIDTH - 1:CONV_WIDTH, :]
    for s in range(1, CONV_WIDTH):
        y = y + ubuf[SUBLANES - s:SUBLANES - s + lc, :] * cw_ref[CONV_WIDTH - 1 - s:CONV_WIDTH - s, :]
    qk = y * jax.nn.sigmoid(y)
    convn_ref[0] = ubuf[SUBLANES + l_valid - (CONV_WIDTH - 1):SUBLANES + l_valid, :]
    ubuf[0:SUBLANES, :] = ubuf[lc:lc + SUBLANES, :]

    row = lax.broadcasted_iota(jnp.int32, (lc, lc), 0)
    col = lax.broadcasted_iota(jnp.int32, (lc, lc), 1)
    tri = col <= row
    g = g_ref[...] + gb_ref[...]
    log_f = jnp.minimum(g, 0.0) - jnp.log1p(jnp.exp(-jnp.abs(g)))
    is_f = (col >= LSTM_HEADS) & (col < 2 * LSTM_HEADS)
    gates = jnp.where(is_f, log_f, g)
    if l_valid < lc:
        gates = jnp.where(row < l_valid, gates, jnp.where(is_f, 0.0, NEG))
    csum = jnp.dot(jnp.where(tri, 1.0, 0.0), jnp.where(is_f, gates, 0.0),
                   precision=_HI, preferred_element_type=F32)
    gates_t = gates.T
    csum_t = csum.T

    for h in range(LSTM_HEADS):
        hs = slice(h * hd, (h + 1) * hd)
        qf = qk[:, hs]
        kf = qk[:, LSTM_WIDTH + h * hd:LSTM_WIDTH + (h + 1) * hd] * (hd ** -0.5)
        vf = v_ref[:, hs]
        qb, kb, vb = qf.astype(BF16), kf.astype(BF16), vf.astype(BF16)
        b_col = csum[:, LSTM_HEADS + h:LSTM_HEADS + h + 1]
        b_row = csum_t[LSTM_HEADS + h:LSTM_HEADS + h + 1, :]
        ig_row = gates_t[h:h + 1, :]
        ig_col = gates[:, h:h + 1]
        m0 = m_ref[0, h:h + 1, 0:1]
        n0 = n_ref[0, h:h + 1, :]
        c0 = c_ref[0, h]

        log_d = jnp.where(tri, b_col - b_row + ig_row, NEG)
        dec0 = b_col + m0
        m_t = jnp.maximum(dec0, jnp.max(log_d, axis=-1, keepdims=True))
        d_m = jnp.exp(log_d - m_t)
        w0 = jnp.exp(dec0 - m_t)
        s = _dot_nt(qb, kb) * d_m
        num = w0 * _dot_nt(qb, c0.astype(BF16)) + _dot(s.astype(BF16), vb)
        den = w0 * jnp.sum(qf * n0, axis=-1, keepdims=True) + jnp.sum(s, axis=-1, keepdims=True)
        hh = num / jnp.maximum(jnp.abs(den), jnp.exp(-m_t))

        m_l = m_t[lc - 1:lc, :]
        b_l = b_col[lc - 1:lc, :]
        w_l = jnp.exp(b_l - b_col + ig_col - m_l)
        w0_l = jnp.exp(b_l + m0 - m_l)
        c_ref[0, h] = w0_l * c0 + _dot((w_l * vf).T.astype(BF16), kb)
        n_ref[0, h:h + 1, :] = w0_l * n0 + jnp.sum(w_l * kf, axis=0, keepdims=True)
        m_ref[0, h:h + 1, :] = jnp.broadcast_to(m_l, (1, LANES))

        hn = hh * lax.rsqrt(jnp.mean(hh * hh, axis=-1, keepdims=True) + NORM_EPS)
        hn = hn * ln_ref[:, hs] * jax.nn.sigmoid(o_ref[:, hs])
        hn_ref[:, hs] = hn.astype(hn_ref.dtype)


def _mlstm(qk, v, o, gates, conv0, c0, n0, m0, conv_w, conv_b, gate_b, lstm_norm, batch, n_chunks, l_valid):
    assert CONV_WIDTH - 1 <= l_valid <= LSTM_CHUNK
    nc = n_chunks
    tok = lambda b, c: (b * nc + c, 0)
    per_b3 = lambda b, c: (b, 0, 0)
    const = lambda b, c: (0, 0)
    rows = batch * nc * LSTM_CHUNK
    return pl.pallas_call(
        functools.partial(_mlstm_body, l_valid),
        grid=(batch, nc),
        in_specs=[pl.BlockSpec((LSTM_CHUNK, 2 * LSTM_WIDTH), tok),
                  pl.BlockSpec((LSTM_CHUNK, LSTM_WIDTH), tok),
                  pl.BlockSpec((LSTM_CHUNK, LSTM_WIDTH), tok),
                  pl.BlockSpec((LSTM_CHUNK, LANES), tok),
                  pl.BlockSpec((SUBLANES, 2 * LSTM_WIDTH), lambda b, c: (b, 0)),
                  pl.BlockSpec((1, LSTM_HEADS, LSTM_HEAD_DIM, LSTM_HEAD_DIM), lambda b, c: (b, 0, 0, 0)),
                  pl.BlockSpec((1, SUBLANES, LANES), per_b3),
                  pl.BlockSpec((1, SUBLANES, LANES), per_b3),
                  pl.BlockSpec((CONV_WIDTH, 2 * LSTM_WIDTH), const),
                  pl.BlockSpec((1, 2 * LSTM_WIDTH), const),
                  pl.BlockSpec((1, LANES), const),
                  pl.BlockSpec((1, LSTM_WIDTH), const)],
        out_specs=[pl.BlockSpec((LSTM_CHUNK, LSTM_WIDTH), tok),
                   pl.BlockSpec((1, CONV_WIDTH - 1, 2 * LSTM_WIDTH), per_b3),
                   pl.BlockSpec((1, LSTM_HEADS, LSTM_HEAD_DIM, LSTM_HEAD_DIM), lambda b, c: (b, 0, 0, 0)),
                   pl.BlockSpec((1, SUBLANES, LANES), per_b3),
                   pl.BlockSpec((1, SUBLANES, LANES), per_b3)],
        out_shape=[jax.ShapeDtypeStruct((rows, LSTM_WIDTH), BF16),
                   jax.ShapeDtypeStruct((batch, CONV_WIDTH - 1, 2 * LSTM_WIDTH), F32),
                   jax.ShapeDtypeStruct((batch, LSTM_HEADS, LSTM_HEAD_DIM, LSTM_HEAD_DIM), F32),
                   jax.ShapeDtypeStruct((batch, SUBLANES, LANES), F32),
                   jax.ShapeDtypeStruct((batch, SUBLANES, LANES), F32)],
        scratch_shapes=[pltpu.VMEM((LSTM_CHUNK + 2 * SUBLANES, 2 * LSTM_WIDTH), F32)],
        compiler_params=_params(("arbitrary", "arbitrary")),
        name="mlstm",
    )(qk, v, o, gates, conv0, c0, n0, m0, conv_w, conv_b, gate_b, lstm_norm)


def _back_body(n_p, ap_ref, as_ref, hp_ref, hs_ref, xp_ref, xs_ref, woa_ref, wob_ref, g_ref, wr_ref, br_ref,
               x1_ref, h2_ref, rt_ref, cnt_ref):
    i = pl.program_id(0)
    is_p = i < n_p
    att = jnp.where(is_p, ap_ref[...], as_ref[...])
    hn = jnp.where(is_p, hp_ref[...], hs_ref[...])
    x = jnp.where(is_p, xp_ref[...], xs_ref[...])
    x1 = x + _dot(att, woa_ref[...]) + _dot(hn, wob_ref[...])
    x1_ref[...] = x1
    ms = jnp.mean(x1 * x1, axis=-1, keepdims=True)
    h2 = (x1 * lax.rsqrt(ms + NORM_EPS) * g_ref[...]).astype(BF16)
    h2_ref[...] = h2
    logits = _dot(h2, wr_ref[...]) + br_ref[...]
    lane = lax.broadcasted_iota(jnp.int32, logits.shape, 1).astype(F32)
    cur = logits
    vals, ids = [], []
    for _ in range(TOP_K):
        mk = jnp.max(cur, axis=-1, keepdims=True)
        ik = jnp.min(jnp.where(cur == mk, lane, float(LANES)), axis=-1, keepdims=True)
        cur = jnp.where(lane == ik, -3e38, cur)
        vals.append(mk)
        ids.append(ik)
    es = [jnp.exp(v - vals[0]) for v in vals]
    tot = es[0]
    for e in es[1:]:
        tot = tot + e
    @pl.when(i == 0)
    def _():
        cnt_ref[...] = jnp.zeros(cnt_ref.shape, F32)

    hits = [lane == ik for ik in ids]
    per_expert = jnp.zeros(logits.shape, F32)
    for hit in hits:
        per_expert = jnp.where(hit, 1.0, per_expert)
    tm = logits.shape[0]
    earlier = jnp.where(lax.broadcasted_iota(jnp.int32, (tm, tm), 1) < lax.broadcasted_iota(jnp.int32, (tm, tm), 0),
                        1.0, 0.0).astype(BF16)
    before = _dot(earlier, per_expert.astype(BF16)) + cnt_ref[0:1, :]
    cnt_ref[0:1, :] = cnt_ref[0:1, :] + jnp.sum(per_expert, axis=0, keepdims=True)

    out = jnp.zeros(logits.shape, F32)
    for k in range(TOP_K):
        out = jnp.where(lane == float(k), ids[k], out)
        out = jnp.where(lane == float(TOP_K + k), es[k] / tot, out)
        rank = jnp.sum(jnp.where(hits[k], before, 0.0), axis=-1, keepdims=True)
        out = jnp.where(lane == float(2 * TOP_K + k), rank, out)
    rt_ref[...] = out


def _back(att_p, att_s, hn_p, hn_s, xp, xs, wo_a, wo_b, g, w_r, b_r):
    t_p, t_s = xp.shape[0], xs.shape[0]
    tm = t_s
    n_p = t_p // tm
    t_all = t_p + t_s
    row = lambda i: (i, 0)
    prow = lambda i: (jnp.minimum(i, n_p - 1), 0)
    const = lambda i: (0, 0)
    return pl.pallas_call(
        functools.partial(_back_body, n_p),
        grid=(n_p + 1,),
        in_specs=[pl.BlockSpec((tm, ATT_WIDTH), prow), pl.BlockSpec((tm, ATT_WIDTH), const),
                  pl.BlockSpec((tm, LSTM_WIDTH), prow), pl.BlockSpec((tm, LSTM_WIDTH), const),
                  pl.BlockSpec((tm, D_MODEL), prow), pl.BlockSpec((tm, D_MODEL), const),
                  pl.BlockSpec((ATT_WIDTH, D_MODEL), const), pl.BlockSpec((LSTM_WIDTH, D_MODEL), const),
                  pl.BlockSpec((1, D_MODEL), const), pl.BlockSpec((D_MODEL, LANES), const),
                  pl.BlockSpec((1, LANES), const)],
        out_specs=[pl.BlockSpec((tm, D_MODEL), row), pl.BlockSpec((tm, D_MODEL), row),
                   pl.BlockSpec((tm, LANES), row)],
        out_shape=[jax.ShapeDtypeStruct((t_all, D_MODEL), F32), jax.ShapeDtypeStruct((t_all, D_MODEL), BF16),
                   jax.ShapeDtypeStruct((t_all, LANES), F32)],
        scratch_shapes=[pltpu.VMEM((SUBLANES, LANES), F32)],
        compiler_params=_params(("arbitrary",)),
        name="back",
    )(att_p, att_s, hn_p, hn_s, xp, xs, wo_a, wo_b, g, w_r, b_r)


def _expert_body(first_block, has_prev, be_ref, nu_ref, x_ref, wu_ref, bu_ref, wd_ref, bd_ref, *rest):
    y_ref, wu_bf, wd_bf = rest[1:] if has_prev else rest
    i = pl.program_id(0)
    g = i + first_block
    live = g < nu_ref[0]

    @pl.when(live & ((i == 0) | (be_ref[g] != be_ref[jnp.maximum(g - 1, 0)])))
    def _():
        wu_bf[...] = wu_ref[0].astype(BF16)
        wd_bf[...] = wd_ref[0].astype(BF16)

    @pl.when(live)
    def _():
        xb = x_ref[...]
        acc = None
        for c in range(D_FF // COL_CHUNK):
            lo, hi = c * COL_CHUNK, (c + 1) * COL_CHUNK
            glu = _dot(xb, wu_bf[:, lo:hi]) + bu_ref[0, :, lo:hi]
            lin = _dot(xb, wu_bf[:, D_FF + lo:D_FF + hi]) + bu_ref[0, :, D_FF + lo:D_FF + hi]
            glu = jnp.minimum(glu, SWIGLU_LIMIT)
            lin = jnp.clip(lin, -SWIGLU_LIMIT, SWIGLU_LIMIT)
            mid = (glu * jax.nn.sigmoid(SWIGLU_ALPHA * glu) * (lin + 1.0)).astype(BF16)
            part = _dot(mid, wd_bf[lo:hi, :])
            acc = part if acc is None else acc + part
        y_ref[...] = (acc + bd_ref[0]).astype(y_ref.dtype)

    @pl.when(jnp.logical_not(live))
    def _():
        y_ref[...] = jnp.zeros(y_ref.shape, y_ref.dtype)


def _experts(x_parts, blk_expert, n_used, w_up, b_up, w_down, b_down):
    n_part = x_parts[0].shape[0] // EXPERT_ROWS
    cap = n_part * EXPERT_ROWS * len(x_parts)
    ys = None
    for p, xs in enumerate(x_parts):
        first = p * n_part
        glob = lambda i, nu, first=first: jnp.minimum(i + first, nu[0] - 1)
        expert = lambda i, be, nu, glob=glob: (be[glob(i, nu)], 0, 0)
        in_specs = [pl.BlockSpec((EXPERT_ROWS, D_MODEL),
                                 lambda i, be, nu, glob=glob, first=first: (jnp.maximum(glob(i, nu) - first, 0), 0)),
                    pl.BlockSpec((1, D_MODEL, 2 * D_FF), expert), pl.BlockSpec((1, 1, 2 * D_FF), expert),
                    pl.BlockSpec((1, D_FF, D_MODEL), expert), pl.BlockSpec((1, 1, D_MODEL), expert)]
        operands = [blk_expert, n_used, xs, w_up, b_up, w_down, b_down]
        if ys is not None:
            in_specs.append(pl.BlockSpec(memory_space=pl.ANY))
            operands.append(ys)
        ys = pl.pallas_call(
            functools.partial(_expert_body, first, p > 0),
            grid_spec=pltpu.PrefetchScalarGridSpec(
                num_scalar_prefetch=2,
                grid=(n_part,),
                in_specs=in_specs,
                out_specs=pl.BlockSpec((EXPERT_ROWS, D_MODEL), lambda i, be, nu, first=first: (i + first, 0)),
                scratch_shapes=[pltpu.VMEM((D_MODEL, 2 * D_FF), BF16), pltpu.VMEM((D_FF, D_MODEL), BF16)]),
            out_shape=jax.ShapeDtypeStruct((cap, D_MODEL), BF16),
            input_output_aliases={len(operands) - 1: 0} if p > 0 else {},
            compiler_params=_params(("arbitrary",), EXPERT_VMEM_LIMIT),
            name="experts",
        )(*operands)
    return ys


def _combine_body(first_tile, n_p, has_prev, has_sample, x1_ref, yg_ref, rt_ref, g_ref, *rest):
    outs = rest[1:] if has_prev else rest
    y = x1_ref[...]
    rt = rt_ref[...]
    for k in range(TOP_K):
        y = y + yg_ref[k] * rt[:, TOP_K + k:TOP_K + k + 1]
    ms = jnp.mean(y * y, axis=-1, keepdims=True)
    out = y * lax.rsqrt(ms + NORM_EPS) * g_ref[...]
    if not has_sample:
        outs[0][...] = out
        return
    tile = pl.program_id(0) + first_tile

    @pl.when(tile < n_p)
    def _():
        outs[0][...] = out

    @pl.when(tile >= n_p)
    def _():
        outs[1][...] = out


def _combine(x1, expert_rows, dest, rt, g, t_p, t_s):
    tm = min(512, t_s)
    assert t_p % tm == 0 and t_s % tm == 0
    n_p, n_s = t_p // tm, t_s // tm
    bounds = [p * n_p // COMBINE_CALLS for p in range(COMBINE_CALLS)] + [n_p + n_s]
    y_p = y_s = None
    for p in range(COMBINE_CALLS):
        lo, hi = bounds[p], bounds[p + 1]
        last = p == COMBINE_CALLS - 1
        yg = expert_rows[dest[:, lo * tm:hi * tm].reshape(-1)].reshape(TOP_K, (hi - lo) * tm, D_MODEL)
        tile = lambda i, lo=lo: (i + lo, 0)
        in_specs = [pl.BlockSpec((tm, D_MODEL), tile), pl.BlockSpec((TOP_K, tm, D_MODEL), lambda i: (0, i, 0)),
                    pl.BlockSpec((tm, LANES), tile), pl.BlockSpec((1, D_MODEL), lambda i: (0, 0))]
        operands = [x1, yg, rt, g]
        if y_p is not None:
            in_specs.append(pl.BlockSpec(memory_space=pl.ANY))
            operands.append(y_p)
        out_specs = [pl.BlockSpec((tm, D_MODEL), lambda i, lo=lo: (jnp.minimum(i + lo, n_p - 1), 0))]
        out_shape = [jax.ShapeDtypeStruct((t_p, D_MODEL), F32)]
        if last:
            out_specs.append(pl.BlockSpec((tm, D_MODEL), lambda i, lo=lo: (jnp.maximum(i + lo - n_p, 0), 0)))
            out_shape.append(jax.ShapeDtypeStruct((t_s, D_MODEL), F32))
        res = pl.pallas_call(
            functools.partial(_combine_body, lo, n_p, p > 0, last),
            grid=(hi - lo,),
            in_specs=in_specs, out_specs=out_specs, out_shape=out_shape,
            input_output_aliases={len(operands) - 1: 0} if p > 0 else {},
            compiler_params=_params(("arbitrary",)),
            name="combine",
        )(*operands)
        y_p = res[0]
        if last:
            y_s = res[1]
    return y_p, y_s


def _route(top_idx, rank):
    t = top_idx.shape[0]
    n = t * TOP_K
    bm = EXPERT_ROWS
    idx_bits = (n - 1).bit_length()
    assert N_EXPERTS << idx_bits < 2 ** 31
    e_flat = top_idx.reshape(-1)
    key = lax.sort(e_flat * (1 << idx_bits) + jnp.arange(n, dtype=jnp.int32))
    order = key & ((1 << idx_bits) - 1)
    experts = jnp.arange(N_EXPERTS, dtype=jnp.int32)
    onehot = e_flat[:, None] == experts[None, :]
    counts = jnp.sum(onehot.astype(jnp.int32), axis=0)
    padded = ((counts + bm - 1) // bm) * bm
    pad_end = jnp.cumsum(padded)
    pad_start = pad_end - padded
    grp_start = jnp.cumsum(counts) - counts
    n_blocks = -(-(-(-n // bm) + N_EXPERTS) // EXPERT_CALLS) * EXPERT_CALLS
    blk_start = jnp.arange(n_blocks, dtype=jnp.int32) * bm
    blk_expert = jnp.minimum(jnp.sum((pad_end[None, :] <= blk_start[:, None]).astype(jnp.int32), axis=1),
                             N_EXPERTS - 1)
    n_used = (pad_end[-1] // bm).astype(jnp.int32).reshape(1)
    shift = (pad_start - grp_start).astype(jnp.int32)
    row_shift = jnp.broadcast_to(shift[blk_expert][:, None], (n_blocks, bm)).reshape(-1)
    row_pos = jnp.clip(jnp.arange(n_blocks * bm, dtype=jnp.int32) - row_shift, 0, n - 1)
    row_tok = order[row_pos] // TOP_K
    dest = jnp.sum(jnp.where(onehot, pad_start[None, :], 0), axis=1) + rank.reshape(-1)
    return row_tok, blk_expert, n_used, dest.reshape(t, TOP_K).T


def kernel(x_prompt, x_sample, cache_k, cache_v, page_table, state_conv, state_C, state_n, state_m,
           norm_mix, w_in, conv_w, conv_b, gate_b, lstm_norm, w_out, norm_ffn, w_router, b_router,
           w_up, b_up, w_down, b_down, norm_final):
    assert w_in.shape[0] == 1, "single-layer stack"
    batch, seq, _ = x_prompt.shape
    db, ds, _ = x_sample.shape
    t_p, t_s = batch * seq, db * ds
    n_pool = cache_k.shape[1]
    assert seq % LSTM_CHUNK == 0 and page_table.shape[1] * PAGE_SIZE % MOBA_BLOCK == 0

    xp = x_prompt.reshape(t_p, D_MODEL)
    xs = x_sample.reshape(t_s, D_MODEL)
    w_main = w_in[0, :, :MAIN_COLS].astype(BF16)
    w_gate = jnp.pad(w_in[0, :, MAIN_COLS:], ((0, 0), (0, LANES - 2 * LSTM_HEADS))).astype(BF16)
    q_a, k_a, v_a, qk_src, v_l, o_l, gts, k_pages, v_pages = _front(xp, xs, norm_mix, w_main, w_gate)

    att_p = _moba(q_a, k_a, v_a, batch, seq)
    k_s, v_s = k_a[t_p:], v_a[t_p:]

    def key_minor(cache):
        cache = cache.reshape(n_pool, ATT_HEADS, PAGE_SIZE, ATT_HEAD_DIM).transpose(0, 1, 3, 2)
        return cache.reshape(n_pool, ATT_WIDTH, PAGE_SIZE)

    att_s = _samp_attn(q_a[t_p:], k_s, v_s, key_minor(cache_k), key_minor(cache_v), page_table, ds).astype(BF16)

    gate_b_row = jnp.pad(gate_b, ((0, 0), (0, LANES - 2 * LSTM_HEADS)))
    cw, cb = conv_w[0], conv_b
    hn_p, conv_p, c_p, n_p_, m_p = _mlstm(
        qk_src, v_l, o_l, gts,
        jnp.zeros((batch * SUBLANES, 2 * LSTM_WIDTH), F32),
        jnp.zeros((batch, LSTM_HEADS, LSTM_HEAD_DIM, LSTM_HEAD_DIM), F32),
        jnp.zeros((batch, SUBLANES, LANES), F32), jnp.zeros((batch, SUBLANES, LANES), F32),
        cw, cb, gate_b_row, lstm_norm, batch, seq // LSTM_CHUNK, LSTM_CHUNK)

    def pad_chunk(a):
        a = a[t_p:].reshape(db, ds, a.shape[-1])
        return jnp.pad(a, ((0, 0), (0, LSTM_CHUNK - ds), (0, 0))).reshape(db * LSTM_CHUNK, a.shape[-1])

    conv0_s = jnp.pad(state_conv[0], ((0, 0), (SUBLANES - (CONV_WIDTH - 1), 0), (0, 0)))
    n0_s = jnp.pad(state_n[0], ((0, 0), (0, SUBLANES - LSTM_HEADS), (0, 0)))
    m0_s = jnp.broadcast_to(jnp.pad(state_m[0], ((0, 0), (0, SUBLANES - LSTM_HEADS)))[:, :, None],
                            (db, SUBLANES, LANES))
    hn_s, conv_s, c_s, n_s, m_s = _mlstm(
        pad_chunk(qk_src), pad_chunk(v_l), pad_chunk(o_l), pad_chunk(gts),
        conv0_s.reshape(db * SUBLANES, 2 * LSTM_WIDTH), state_C[0], n0_s, m0_s,
        cw, cb, gate_b_row, lstm_norm, db, 1, ds)
    hn_s = hn_s.reshape(db, LSTM_CHUNK, LSTM_WIDTH)[:, :ds].reshape(t_s, LSTM_WIDTH)

    wo = w_out[0].astype(BF16)
    w_r = jnp.pad(w_router[0], ((0, 0), (0, LANES - N_EXPERTS))).astype(BF16)
    b_r = jnp.pad(b_router, ((0, 0), (0, LANES - N_EXPERTS)), constant_values=NEG)
    x1, h2, rt = _back(att_p, att_s, hn_p, hn_s, xp, xs, wo[:ATT_WIDTH], wo[ATT_WIDTH:], norm_ffn, w_r, b_r)

    row_tok, blk_expert, n_used, dest = _route(rt[:, :TOP_K].astype(jnp.int32),
                                               rt[:, 2 * TOP_K:3 * TOP_K].astype(jnp.int32))
    x_parts = [h2[piece] for piece in jnp.split(row_tok, EXPERT_CALLS)]
    ys = _experts(x_parts, blk_expert, n_used, w_up[0], b_up[0][:, None, :], w_down[0], b_down[0][:, None, :])
    y_p, y_s = _combine(x1, ys, dest, rt, norm_final.reshape(1, D_MODEL), t_p, t_s)

    def to_pages(t):
        t = t.reshape(batch, seq // PAGE_SIZE, ATT_HEADS, ATT_HEAD_DIM, PAGE_SIZE)
        return t.transpose(0, 1, 2, 4, 3)[None]

    def to_slots(t):
        return t.reshape(db, ds, ATT_HEADS, ATT_HEAD_DIM).transpose(0, 2, 1, 3)[None]

    return (y_p.reshape(batch, seq, D_MODEL), y_s.reshape(db, ds, D_MODEL),
            to_pages(k_pages), to_pages(v_pages), to_slots(k_s), to_slots(v_s),
            conv_p[None], conv_s[None],
            c_p[None], n_p_[None, :, :LSTM_HEADS, :], m_p[None, :, :LSTM_HEADS, 0],
            c_s[None], n_s[None, :, :LSTM_HEADS, :], m_s[None, :, :LSTM_HEADS, 0])
```

```python
import functools

import jax
import jax.numpy as jnp
from jax import lax
from jax.experimental import pallas as pl
from jax.experimental.pallas import tpu as pltpu

F32 = jnp.float32
BF16 = jnp.bfloat16

D_MODEL = 1024
ATT_HEADS = 8
ATT_HEAD_DIM = 64
ATT_WIDTH = ATT_HEADS * ATT_HEAD_DIM
MOBA_BLOCK = 256
MOBA_TOPK = 3
LSTM_HEADS = 4
LSTM_HEAD_DIM = 128
LSTM_WIDTH = LSTM_HEADS * LSTM_HEAD_DIM
CONV_WIDTH = 4
LSTM_CHUNK = 128
PAGE_SIZE = 128
N_EXPERTS = 32
TOP_K = 4
D_FF = D_MODEL
SWIGLU_LIMIT = 7.0
SWIGLU_ALPHA = 1.702
NORM_EPS = 1e-5
MAIN_COLS = 3 * ATT_WIDTH + 4 * LSTM_WIDTH
COL_CHUNK = 512
LANES = 128
SUBLANES = 8
EXPERT_ROWS = 512
EXPERT_CALLS = 4
NEG = -1e30
LOG2E = 1.4426950408889634
N_FEAT = 4
SCORE_AHEAD = 4
VMEM_LIMIT = 48 * 1024 * 1024
EXPERT_VMEM_LIMIT = 56 * 1024 * 1024

_NT = (((1,), (1,)), ((), ()))
_HI = lax.Precision.HIGHEST


def _dot(a, b):
    return jnp.dot(a, b, preferred_element_type=F32)


def _dot_nt(a, b, precision=None):
    return lax.dot_general(a, b, _NT, precision=precision, preferred_element_type=F32)


def _params(sem, vmem_limit=VMEM_LIMIT):
    return pltpu.CompilerParams(dimension_semantics=sem, vmem_limit_bytes=vmem_limit)


def _front_body(n_p, xp_ref, xs_ref, g_ref, w_ref, wg_ref,
                qa_ref, ka_ref, va_ref, qk_ref, vl_ref, ol_ref, gt_ref, kpg_ref, vpg_ref):
    del n_p
    x = jnp.where(pl.program_id(0) == 0, xs_ref[...], xp_ref[...])
    ms = jnp.mean(x * x, axis=-1, keepdims=True)
    h = (x * lax.rsqrt(ms + NORM_EPS) * g_ref[...]).astype(BF16)
    dests = ((qa_ref, 0, None), (ka_ref, 0, kpg_ref), (va_ref, 0, vpg_ref), (qk_ref, 0, None), (qk_ref, 1, None),
             (vl_ref, 0, None), (ol_ref, 0, None))
    for c, (ref, part, pages_ref) in enumerate(dests):
        z = _dot(h, w_ref[:, c * COL_CHUNK:(c + 1) * COL_CHUNK])
        ref[:, part * COL_CHUNK:(part + 1) * COL_CHUNK] = z.astype(ref.dtype)
        if pages_ref is not None:
            for pg in range(pages_ref.shape[0]):
                pages_ref[pg] = z[pg * PAGE_SIZE:(pg + 1) * PAGE_SIZE, :].T
    gt_ref[...] = _dot(h, wg_ref[...])


def _front(xp, xs, g, w_main, w_gate):
    t_p, t_s = xp.shape[0], xs.shape[0]
    tm = t_s
    assert t_p % tm == 0 and tm % 128 == 0
    n_p = t_p // tm
    t_all = t_p + t_s
    prompt = lambda i: jnp.maximum(i - 1, 0)
    row = lambda i: (jnp.where(i == 0, n_p, i - 1), 0)
    const = lambda i: (0, 0)
    wide = lambda n, dt: jax.ShapeDtypeStruct((t_all, n), dt)
    ppt = tm // PAGE_SIZE
    pages = pl.BlockSpec((ppt, ATT_WIDTH, PAGE_SIZE), lambda i: (prompt(i), 0, 0))
    pages_shape = jax.ShapeDtypeStruct((t_p // PAGE_SIZE, ATT_WIDTH, PAGE_SIZE), F32)
    return pl.pallas_call(
        functools.partial(_front_body, n_p),
        grid=(n_p + 1,),
        in_specs=[pl.BlockSpec((tm, D_MODEL), lambda i: (prompt(i), 0)),
                  pl.BlockSpec((tm, D_MODEL), const),
                  pl.BlockSpec((1, D_MODEL), const),
                  pl.BlockSpec((D_MODEL, MAIN_COLS), const),
                  pl.BlockSpec((D_MODEL, LANES), const)],
        out_specs=[pl.BlockSpec((tm, ATT_WIDTH), row), pl.BlockSpec((tm, ATT_WIDTH), row),
                   pl.BlockSpec((tm, ATT_WIDTH), row), pl.BlockSpec((tm, 2 * LSTM_WIDTH), row),
                   pl.BlockSpec((tm, LSTM_WIDTH), row), pl.BlockSpec((tm, LSTM_WIDTH), row),
                   pl.BlockSpec((tm, LANES), row), pages, pages],
        out_shape=[wide(ATT_WIDTH, BF16), wide(ATT_WIDTH, F32), wide(ATT_WIDTH, F32),
                   wide(2 * LSTM_WIDTH, F32), wide(LSTM_WIDTH, F32), wide(LSTM_WIDTH, F32),
                   wide(LANES, F32), pages_shape, pages_shape],
        compiler_params=_params(("arbitrary",)),
        name="front",
    )(xp, xs, g, w_main, w_gate)


def _topk_rows(g, n_valid, k_sel):
    n = g.shape[0]
    jj = lax.broadcasted_iota(jnp.int32, g.shape, 0)
    rank = jnp.zeros(g.shape, F32)
    for j2 in range(n):
        row = g[j2:j2 + 1, :]
        beats = jnp.where(row > g, 1.0, jnp.where((row == g) & (jj > j2), 1.0, 0.0))
        rank = rank + jnp.where(j2 < n_valid, beats, 0.0)
    return jnp.where((jj < n_valid) & (rank < k_sel), 1.0, 0.0)


def _topk_lanes(g, n, k_sel):
    jj = lax.broadcasted_iota(jnp.int32, g.shape, 1)
    rank = jnp.zeros(g.shape, F32)
    for j2 in range(n):
        col = g[:, j2:j2 + 1]
        rank = rank + jnp.where(col > g, 1.0, jnp.where((col == g) & (jj > j2), 1.0, 0.0))
    return jnp.where((jj < n) & (rank < k_sel), 1.0, 0.0)


def _moba_body(nb, q_ref, k_ref, v_ref, o_ref, kaug, vt, kmean):
    i = pl.program_id(1)
    blk = MOBA_BLOCK
    hd = ATT_HEAD_DIM
    seq = nb * blk
    k_sel = min(MOBA_TOPK, nb - 1)
    heads_per = LANES // hd

    @pl.when(i == 0)
    def _():
        row = lax.broadcasted_iota(jnp.int32, (seq, LANES), 0)
        lane_s = lax.broadcasted_iota(jnp.int32, (seq, LANES), 1)
        for c in range(ATT_WIDTH // LANES):
            kf = k_ref[:, c * LANES:(c + 1) * LANES]
            kmean[c] = jnp.sum(kf.reshape(nb, blk, LANES), axis=1) * (1.0 / blk)
            for hl in range(heads_per):
                f = lane_s - (heads_per - 1 - hl) * hd
                feat = jnp.where(f == 0, (row % blk).astype(F32),
                                 jnp.where(f == 1, (row // blk * blk).astype(F32),
                                           jnp.where((f == 2) | (f == 3), 1.0,
                                                     jnp.where(f - N_FEAT == row // blk, 1.0, 0.0))))
                kaug[c * heads_per + hl] = jnp.where((lane_s // hd) == hl, kf, feat).astype(BF16)
        for jb in range(nb):
            vt[jb] = v_ref[jb * blk:(jb + 1) * blk, :].T.astype(BF16)

    key_io = lax.broadcasted_iota(jnp.int32, (blk, blk), 0)
    qry_io = lax.broadcasted_iota(jnp.int32, (blk, blk), 1)
    causal = key_io <= qry_io
    q_row = lax.broadcasted_iota(jnp.int32, (blk, LANES), 0)
    lane = lax.broadcasted_iota(jnp.int32, (blk, LANES), 1)
    blk_io = lax.broadcasted_iota(jnp.int32, (nb, blk), 0)
    own = pl.multiple_of(i * blk, blk)

    q_augs = []
    for hp in range(ATT_WIDTH // LANES):
        sl = slice(hp * LANES, (hp + 1) * LANES)
        q2 = q_ref[:, sl].astype(F32) * (hd ** -0.5)
        km = kmean[hp]
        for hl in range(heads_per):
            head = hp * heads_per + hl
            slope = float(2.0 ** (-8.0 * (head + 1) / ATT_HEADS))
            inhead = (lane // hd) == hl
            gate_t = _dot_nt(km, jnp.where(inhead, q2, 0.0), _HI)
            keep = (_topk_rows(gate_t, i, k_sel) > 0.5) | (blk_io == i)
            spare = (heads_per - 1 - hl) * hd + N_FEAT
            mask_t = jnp.concatenate([jnp.zeros((spare, blk), F32), jnp.where(keep, 0.0, NEG),
                                      jnp.zeros((LANES - spare - nb, blk), F32)], axis=0)
            f = lane - (heads_per - 1 - hl) * hd
            feat = jnp.where(f <= 1, slope,
                             jnp.where(f == 2, -slope * (i * blk).astype(F32),
                                       jnp.where(f == 3, -slope * q_row.astype(F32), mask_t.T)))
            q_augs.append(jnp.where(inhead, q2, feat).astype(BF16))

    def block_pass(j, off, carry):
        def scores(h):
            return _dot_nt(kaug[h, pl.ds(off, blk), :], q_augs[h]) * LOG2E

        ahead = {h: scores(h) for h in range(min(SCORE_AHEAD, ATT_HEADS))}
        out = ()
        for h in range(ATT_HEADS):
            t = ahead.pop(h)
            if carry is None:
                t = jnp.where(causal, t, NEG)
                m_new = jnp.max(t, axis=0, keepdims=True)
            else:
                m, l, acc = carry[3 * h:3 * h + 3]
                m_new = jnp.maximum(m, jnp.max(t, axis=0, keepdims=True))
            p = jnp.exp2(t - m_new)
            if h + SCORE_AHEAD < ATT_HEADS:
                ahead[h + SCORE_AHEAD] = scores(h + SCORE_AHEAD)
            pv = _dot(vt[j, h * hd:(h + 1) * hd, :], p.astype(BF16))
            if carry is None:
                out = out + (m_new, jnp.sum(p, axis=0, keepdims=True), pv)
            else:
                a = jnp.exp2(m - m_new)
                out = out + (m_new, a * l + jnp.sum(p, axis=0, keepdims=True), a * acc + pv)
        return out

    carry = block_pass(i, own, None)
    carry = lax.fori_loop(0, i, lambda j, c: block_pass(j, pl.multiple_of(j * blk, blk), c), carry)
    for hp in range(ATT_WIDTH // LANES):
        o_t = jnp.concatenate([carry[3 * h + 2] / carry[3 * h + 1]
                               for h in range(hp * heads_per, (hp + 1) * heads_per)], axis=0)
        o_ref[:, hp * LANES:(hp + 1) * LANES] = o_t.T.astype(o_ref.dtype)


def _moba(q_all, k_all, v_all, batch, seq):
    nb = seq // MOBA_BLOCK
    assert seq % MOBA_BLOCK == 0 and N_FEAT + nb <= ATT_HEAD_DIM and ATT_HEADS == 8
    return pl.pallas_call(
        functools.partial(_moba_body, nb),
        grid=(batch, nb),
        in_specs=[pl.BlockSpec((MOBA_BLOCK, ATT_WIDTH), lambda b, i: (b * nb + i, 0)),
                  pl.BlockSpec((seq, ATT_WIDTH), lambda b, i: (b, 0)),
                  pl.BlockSpec((seq, ATT_WIDTH), lambda b, i: (b, 0))],
        out_specs=pl.BlockSpec((MOBA_BLOCK, ATT_WIDTH), lambda b, i: (b * nb + i, 0)),
        out_shape=jax.ShapeDtypeStruct((batch * seq, ATT_WIDTH), BF16),
        scratch_shapes=[pltpu.VMEM((ATT_HEADS, seq, LANES), BF16), pltpu.VMEM((nb, ATT_WIDTH, MOBA_BLOCK), BF16),
                        pltpu.VMEM((ATT_WIDTH // LANES, nb, LANES), F32)],
        compiler_params=_params(("arbitrary", "arbitrary")),
        name="moba",
    )(q_all, k_all, v_all)


def _samp_attn_body(n_blk, bps, past_len, dec_seq, pt_ref, qbd_ref, q_ref, kn_ref, vn_ref, *rest):
    del pt_ref
    ppb = MOBA_BLOCK // PAGE_SIZE
    k_refs, v_refs = rest[:ppb * bps], rest[ppb * bps:2 * ppb * bps]
    o_ref, m_scr, l_scr, acc_scr, g_scr = rest[2 * ppb * bps:]
    step = pl.program_id(1)
    hq = ATT_HEADS * SUBLANES
    scale = ATT_HEAD_DIM ** -0.5
    qbd = qbd_ref[0]
    wide = bps * MOBA_BLOCK
    rows = lax.broadcasted_iota(jnp.int32, (hq, wide), 0)
    slope = jnp.exp2((rows // SUBLANES + 1).astype(F32) * (-8.0 / ATT_HEADS))
    tq = (past_len + rows % SUBLANES).astype(F32)
    pos = (step * wide + lax.broadcasted_iota(jnp.int32, (hq, wide), 1)).astype(F32)
    lanes = lax.broadcasted_iota(jnp.int32, (hq, LANES), 1)

    @pl.when(step == 0)
    def _():
        g_scr[...] = jnp.zeros(g_scr.shape, F32)
        m_scr[...] = jnp.zeros(m_scr.shape, F32)
        l_scr[...] = jnp.zeros(l_scr.shape, F32)

    k_cat = jnp.concatenate([r[...].astype(BF16) for r in k_refs], axis=1)
    s_raw = _dot(qbd, k_cat)
    s = s_raw * scale - slope * (tq - pos)
    ones = jnp.ones((MOBA_BLOCK, LANES), BF16)
    g_all, m_all, l_all = g_scr[...], m_scr[...], l_scr[...]
    for bl in range(bps):
        j = step * bps + bl
        seg = slice(bl * MOBA_BLOCK, (bl + 1) * MOBA_BLOCK)
        m = jnp.max(s[:, seg], axis=-1, keepdims=True)
        p = jnp.exp(s[:, seg] - m).astype(BF16)
        l_rep = _dot(p, ones)
        v_cat = jnp.concatenate([v_refs[ppb * bl + pg][...].astype(BF16) for pg in range(ppb)], axis=1)
        acc_scr[j] = _dot_nt(p, v_cat)
        gate = jnp.sum(s_raw[:, seg], axis=-1, keepdims=True) * (1.0 / MOBA_BLOCK)
        hit = lanes == j
        g_all = jnp.where(hit, gate, g_all)
        m_all = jnp.where(hit, m, m_all)
        l_all = jnp.where(hit, l_rep, l_all)
    g_scr[...] = g_all
    m_scr[...] = m_all
    l_scr[...] = l_all

    @pl.when(step == n_blk // bps - 1)
    def _():
        chosen = _topk_lanes(g_all, n_blk, min(MOBA_TOPK, n_blk)) > 0.5

        r2 = lax.broadcasted_iota(jnp.int32, (hq, hq), 0)
        c2 = lax.broadcasted_iota(jnp.int32, (hq, hq), 1)
        ok = ((c2 // SUBLANES) == (r2 // SUBLANES)) & ((c2 % SUBLANES) <= (r2 % SUBLANES)) \
            & ((c2 % SUBLANES) < dec_seq)
        slope2 = jnp.exp2((r2 // SUBLANES + 1).astype(F32) * (-8.0 / ATT_HEADS))
        s = _dot_nt(q_ref[0], kn_ref[0]) * scale - slope2 * (r2 % SUBLANES - c2 % SUBLANES).astype(F32)
        s = jnp.where(ok, s, NEG)
        m_loc = jnp.max(s, axis=-1, keepdims=True)
        p = jnp.exp(s - m_loc)
        l_loc = jnp.sum(p, axis=-1, keepdims=True)
        a_loc = _dot(p.astype(BF16), vn_ref[0])

        m_tot = jnp.maximum(m_loc, jnp.max(jnp.where(chosen, m_all, NEG), axis=-1, keepdims=True))
        w_blk = jnp.where(chosen, jnp.exp(m_all - m_tot), 0.0)
        w_loc = jnp.exp(m_loc - m_tot)
        den = w_loc * l_loc + jnp.sum(w_blk * l_all, axis=-1, keepdims=True)
        num_w = jnp.zeros((hq, ATT_WIDTH), F32)
        for jb in range(n_blk):
            num_w = num_w + w_blk[:, jb:jb + 1] * acc_scr[jb]
        num = jnp.concatenate(
            [num_w[h * SUBLANES:(h + 1) * SUBLANES, h * ATT_HEAD_DIM:(h + 1) * ATT_HEAD_DIM]
             for h in range(ATT_HEADS)], axis=0)
        o_ref[0] = (num + w_loc * a_loc) / den


def _samp_attn(q_s, k_s, v_s, cache_kt, cache_vt, page_table, dec_seq):
    db, n_pages = page_table.shape
    ppb = MOBA_BLOCK // PAGE_SIZE
    assert ppb == 2 and n_pages % ppb == 0 and dec_seq <= SUBLANES
    n_blk = n_pages // ppb
    assert 1 <= n_blk <= LANES
    bps = next(c for c in (16, 8, 4, 2, 1) if n_blk % c == 0)
    past_len = n_pages * PAGE_SIZE
    hq = ATT_HEADS * SUBLANES

    def rows(t):
        t = t.reshape(db, dec_seq, ATT_HEADS, ATT_HEAD_DIM).transpose(0, 2, 1, 3)
        return jnp.pad(t, ((0, 0), (0, 0), (0, SUBLANES - dec_seq), (0, 0)))

    flat = lambda t: t.reshape(db, hq, ATT_HEAD_DIM).astype(BF16)
    q_r = rows(q_s)
    eye = jnp.eye(ATT_HEADS, dtype=q_r.dtype)
    q_bd = (q_r[:, :, :, None, :] * eye[None, :, None, :, None]).reshape(db, hq, ATT_WIDTH).astype(BF16)

    small = pl.BlockSpec((1, hq, ATT_HEAD_DIM), lambda b, s, pt: (b, 0, 0))

    def page(k):
        return pl.BlockSpec((None, ATT_WIDTH, PAGE_SIZE),
                            lambda b, s, pt: (pt[b * n_pages + ppb * bps * s + k], 0, 0))

    pages = [page(k) for k in range(ppb * bps)]
    out = pl.pallas_call(
        functools.partial(_samp_attn_body, n_blk, bps, past_len, dec_seq),
        grid_spec=pltpu.PrefetchScalarGridSpec(
            num_scalar_prefetch=1,
            grid=(db, n_blk // bps),
            in_specs=[pl.BlockSpec((1, hq, ATT_WIDTH), lambda b, s, pt: (b, 0, 0)), small, small, small]
            + pages + pages,
            out_specs=small,
            scratch_shapes=[pltpu.VMEM((hq, LANES), F32), pltpu.VMEM((hq, LANES), F32),
                            pltpu.VMEM((n_blk, hq, ATT_WIDTH), F32), pltpu.VMEM((hq, LANES), F32)]),
        out_shape=jax.ShapeDtypeStruct((db, hq, ATT_HEAD_DIM), F32),
        compiler_params=_params(("arbitrary", "arbitrary")),
        name="samp_attn",
    )(page_table.reshape(-1), q_bd, flat(q_r), flat(rows(k_s)), flat(rows(v_s)),
      *([cache_kt] * (ppb * bps)), *([cache_vt] * (ppb * bps)))
    out = out.reshape(db, ATT_HEADS, SUBLANES, ATT_HEAD_DIM)[:, :, :dec_seq]
    return out.transpose(0, 2, 1, 3).reshape(db * dec_seq, ATT_WIDTH)


def _mlstm_body(l_valid, qk_ref, v_ref, o_ref, g_ref, conv0_ref, c0_ref, n0_ref, m0_ref,
                cw_ref, cb_ref, gb_ref, ln_ref,
                hn_ref, convn_ref, c_ref, n_ref, m_ref, ubuf):
    c = pl.program_id(1)
    lc = LSTM_CHUNK
    hd = LSTM_HEAD_DIM

    @pl.when(c == 0)
    def _():
        ubuf[0:SUBLANES, :] = conv0_ref[...]
        c_ref[...] = c0_ref[...]
        n_ref[...] = n0_ref[...]
        m_ref[...] = m0_ref[...]

    u = qk_ref[...]
    ubuf[SUBLANES:SUBLANES + lc, :] = u
    y = cb_ref[...] + u * cw_ref[CONV_WIDTH - 1:CONV_WIDTH, :]
    for s in range(1, CONV_WIDTH):
        y = y + ubuf[SUBLANES - s:SUBLANES - s + lc, :] * cw_ref[CONV_WIDTH - 1 - s:CONV_WIDTH - s, :]
    qk = y * jax.nn.sigmoid(y)
    convn_ref[0] = ubuf[SUBLANES + l_valid - (CONV_WIDTH - 1):SUBLANES + l_valid, :]
    ubuf[0:SUBLANES, :] = ubuf[lc:lc + SUBLANES, :]

    row = lax.broadcasted_iota(jnp.int32, (lc, lc), 0)
    col = lax.broadcasted_iota(jnp.int32, (lc, lc), 1)
    tri = col <= row
    g = g_ref[...] + gb_ref[...]
    log_f = jnp.minimum(g, 0.0) - jnp.log1p(jnp.exp(-jnp.abs(g)))
    is_f = (col >= LSTM_HEADS) & (col < 2 * LSTM_HEADS)
    gates = jnp.where(is_f, log_f, g)
    if l_valid < lc:
        gates = jnp.where(row < l_valid, gates, jnp.where(is_f, 0.0, NEG))
    csum = jnp.dot(jnp.where(tri, 1.0, 0.0), jnp.where(is_f, gates, 0.0),
                   precision=_HI, preferred_element_type=F32)
    gates_t = gates.T
    csum_t = csum.T

    for h in range(LSTM_HEADS):
        hs = slice(h * hd, (h + 1) * hd)
        qf = qk[:, hs]
        kf = qk[:, LSTM_WIDTH + h * hd:LSTM_WIDTH + (h + 1) * hd] * (hd ** -0.5)
        vf = v_ref[:, hs]
        qb, kb, vb = qf.astype(BF16), kf.astype(BF16), vf.astype(BF16)
        b_col = csum[:, LSTM_HEADS + h:LSTM_HEADS + h + 1]
        b_row = csum_t[LSTM_HEADS + h:LSTM_HEADS + h + 1, :]
        ig_row = gates_t[h:h + 1, :]
        ig_col = gates[:, h:h + 1]
        m0 = m_ref[0, h:h + 1, 0:1]
        n0 = n_ref[0, h:h + 1, :]
        c0 = c_ref[0, h]

        log_d = jnp.where(tri, b_col - b_row + ig_row, NEG)
        dec0 = b_col + m0
        m_t = jnp.maximum(dec0, jnp.max(log_d, axis=-1, keepdims=True))
        d_m = jnp.exp(log_d - m_t)
        w0 = jnp.exp(dec0 - m_t)
        s = _dot_nt(qb, kb) * d_m
        num = w0 * _dot_nt(qb, c0.astype(BF16)) + _dot(s.astype(BF16), vb)
        den = w0 * jnp.sum(qf * n0, axis=-1, keepdims=True) + jnp.sum(s, axis=-1, keepdims=True)
        hh = num / jnp.maximum(jnp.abs(den), jnp.exp(-m_t))

        m_l = m_t[lc - 1:lc, :]
        b_l = b_col[lc - 1:lc, :]
        w_l = jnp.exp(b_l - b_col + ig_col - m_l)
        w0_l = jnp.exp(b_l + m0 - m_l)
        c_ref[0, h] = w0_l * c0 + _dot((w_l * vf).T.astype(BF16), kb)
        n_ref[0, h:h + 1, :] = w0_l * n0 + jnp.sum(w_l * kf, axis=0, keepdims=True)
        m_ref[0, h:h + 1, :] = jnp.broadcast_to(m_l, (1, LANES))

        hn = hh * lax.rsqrt(jnp.mean(hh * hh, axis=-1, keepdims=True) + NORM_EPS)
        hn = hn * ln_ref[:, hs] * jax.nn.sigmoid(o_ref[:, hs])
        hn_ref[:, hs] = hn.astype(hn_ref.dtype)


def _mlstm(qk, v, o, gates, conv0, c0, n0, m0, conv_w, conv_b, gate_b, lstm_norm, batch, n_chunks, l_valid):
    assert CONV_WIDTH - 1 <= l_valid <= LSTM_CHUNK
    nc = n_chunks
    tok = lambda b, c: (b * nc + c, 0)
    per_b3 = lambda b, c: (b, 0, 0)
    const = lambda b, c: (0, 0)
    rows = batch * nc * LSTM_CHUNK
    return pl.pallas_call(
        functools.partial(_mlstm_body, l_valid),
        grid=(batch, nc),
        in_specs=[pl.BlockSpec((LSTM_CHUNK, 2 * LSTM_WIDTH), tok),
                  pl.BlockSpec((LSTM_CHUNK, LSTM_WIDTH), tok),
                  pl.BlockSpec((LSTM_CHUNK, LSTM_WIDTH), tok),
                  pl.BlockSpec((LSTM_CHUNK, LANES), tok),
                  pl.BlockSpec((SUBLANES, 2 * LSTM_WIDTH), lambda b, c: (b, 0)),
                  pl.BlockSpec((1, LSTM_HEADS, LSTM_HEAD_DIM, LSTM_HEAD_DIM), lambda b, c: (b, 0, 0, 0)),
                  pl.BlockSpec((1, SUBLANES, LANES), per_b3),
                  pl.BlockSpec((1, SUBLANES, LANES), per_b3),
                  pl.BlockSpec((CONV_WIDTH, 2 * LSTM_WIDTH), const),
                  pl.BlockSpec((1, 2 * LSTM_WIDTH), const),
                  pl.BlockSpec((1, LANES), const),
                  pl.BlockSpec((1, LSTM_WIDTH), const)],
        out_specs=[pl.BlockSpec((LSTM_CHUNK, LSTM_WIDTH), tok),
                   pl.BlockSpec((1, CONV_WIDTH - 1, 2 * LSTM_WIDTH), per_b3),
                   pl.BlockSpec((1, LSTM_HEADS, LSTM_HEAD_DIM, LSTM_HEAD_DIM), lambda b, c: (b, 0, 0, 0)),
                   pl.BlockSpec((1, SUBLANES, LANES), per_b3),
                   pl.BlockSpec((1, SUBLANES, LANES), per_b3)],
        out_shape=[jax.ShapeDtypeStruct((rows, LSTM_WIDTH), BF16),
                   jax.ShapeDtypeStruct((batch, CONV_WIDTH - 1, 2 * LSTM_WIDTH), F32),
                   jax.ShapeDtypeStruct((batch, LSTM_HEADS, LSTM_HEAD_DIM, LSTM_HEAD_DIM), F32),
                   jax.ShapeDtypeStruct((batch, SUBLANES, LANES), F32),
                   jax.ShapeDtypeStruct((batch, SUBLANES, LANES), F32)],
        scratch_shapes=[pltpu.VMEM((LSTM_CHUNK + 2 * SUBLANES, 2 * LSTM_WIDTH), F32)],
        compiler_params=_params(("arbitrary", "arbitrary")),
        name="mlstm",
    )(qk, v, o, gates, conv0, c0, n0, m0, conv_w, conv_b, gate_b, lstm_norm)


def _back_body(n_p, ap_ref, as_ref, hp_ref, hs_ref, xp_ref, xs_ref, woa_ref, wob_ref, g_ref, wr_ref, br_ref,
               x1_ref, h2_ref, rt_ref, cnt_ref):
    i = pl.program_id(0)
    is_p = i < n_p
    att = jnp.where(is_p, ap_ref[...], as_ref[...])
    hn = jnp.where(is_p, hp_ref[...], hs_ref[...])
    x = jnp.where(is_p, xp_ref[...], xs_ref[...])
    x1 = x + _dot(att, woa_ref[...]) + _dot(hn, wob_ref[...])
    x1_ref[...] = x1
    ms = jnp.mean(x1 * x1, axis=-1, keepdims=True)
    h2 = (x1 * lax.rsqrt(ms + NORM_EPS) * g_ref[...]).astype(BF16)
    h2_ref[...] = h2
    logits = _dot(h2, wr_ref[...]) + br_ref[...]
    lane = lax.broadcasted_iota(jnp.int32, logits.shape, 1).astype(F32)
    cur = logits
    vals, ids = [], []
    for _ in range(TOP_K):
        mk = jnp.max(cur, axis=-1, keepdims=True)
        ik = jnp.min(jnp.where(cur == mk, lane, float(LANES)), axis=-1, keepdims=True)
        cur = jnp.where(lane == ik, -3e38, cur)
        vals.append(mk)
        ids.append(ik)
    es = [jnp.exp(v - vals[0]) for v in vals]
    tot = es[0]
    for e in es[1:]:
        tot = tot + e
    @pl.when(i == 0)
    def _():
        cnt_ref[...] = jnp.zeros(cnt_ref.shape, F32)

    hits = [lane == ik for ik in ids]
    per_expert = jnp.zeros(logits.shape, F32)
    for hit in hits:
        per_expert = jnp.where(hit, 1.0, per_expert)
    tm = logits.shape[0]
    earlier = jnp.where(lax.broadcasted_iota(jnp.int32, (tm, tm), 1) < lax.broadcasted_iota(jnp.int32, (tm, tm), 0),
                        1.0, 0.0).astype(BF16)
    before = _dot(earlier, per_expert.astype(BF16)) + cnt_ref[0:1, :]
    cnt_ref[0:1, :] = cnt_ref[0:1, :] + jnp.sum(per_expert, axis=0, keepdims=True)

    out = jnp.zeros(logits.shape, F32)
    for k in range(TOP_K):
        out = jnp.where(lane == float(k), ids[k], out)
        out = jnp.where(lane == float(TOP_K + k), es[k] / tot, out)
        rank = jnp.sum(jnp.where(hits[k], before, 0.0), axis=-1, keepdims=True)
        out = jnp.where(lane == float(2 * TOP_K + k), rank, out)
    rt_ref[...] = out


def _back(att_p, att_s, hn_p, hn_s, xp, xs, wo_a, wo_b, g, w_r, b_r):
    t_p, t_s = xp.shape[0], xs.shape[0]
    tm = t_s
    n_p = t_p // tm
    t_all = t_p + t_s
    row = lambda i: (i, 0)
    prow = lambda i: (jnp.minimum(i, n_p - 1), 0)
    const = lambda i: (0, 0)
    return pl.pallas_call(
        functools.partial(_back_body, n_p),
        grid=(n_p + 1,),
        in_specs=[pl.BlockSpec((tm, ATT_WIDTH), prow), pl.BlockSpec((tm, ATT_WIDTH), const),
                  pl.BlockSpec((tm, LSTM_WIDTH), prow), pl.BlockSpec((tm, LSTM_WIDTH), const),
                  pl.BlockSpec((tm, D_MODEL), prow), pl.BlockSpec((tm, D_MODEL), const),
                  pl.BlockSpec((ATT_WIDTH, D_MODEL), const), pl.BlockSpec((LSTM_WIDTH, D_MODEL), const),
                  pl.BlockSpec((1, D_MODEL), const), pl.BlockSpec((D_MODEL, LANES), const),
                  pl.BlockSpec((1, LANES), const)],
        out_specs=[pl.BlockSpec((tm, D_MODEL), row), pl.BlockSpec((tm, D_MODEL), row),
                   pl.BlockSpec((tm, LANES), row)],
        out_shape=[jax.ShapeDtypeStruct((t_all, D_MODEL), F32), jax.ShapeDtypeStruct((t_all, D_MODEL), BF16),
                   jax.ShapeDtypeStruct((t_all, LANES), F32)],
        scratch_shapes=[pltpu.VMEM((SUBLANES, LANES), F32)],
        compiler_params=_params(("arbitrary",)),
        name="back",
    )(att_p, att_s, hn_p, hn_s, xp, xs, wo_a, wo_b, g, w_r, b_r)


def _expert_body(first_block, has_prev, be_ref, nu_ref, x_ref, wu_ref, bu_ref, wd_ref, bd_ref, *rest):
    y_ref, wu_bf, wd_bf = rest[1:] if has_prev else rest
    i = pl.program_id(0)
    g = i + first_block
    live = g < nu_ref[0]

    @pl.when(live & ((i == 0) | (be_ref[g] != be_ref[jnp.maximum(g - 1, 0)])))
    def _():
        wu_bf[...] = wu_ref[0].astype(BF16)
        wd_bf[...] = wd_ref[0].astype(BF16)

    @pl.when(live)
    def _():
        xb = x_ref[...]
        acc = None
        for c in range(D_FF // COL_CHUNK):
            lo, hi = c * COL_CHUNK, (c + 1) * COL_CHUNK
            glu = _dot(xb, wu_bf[:, lo:hi]) + bu_ref[0, :, lo:hi]
            lin = _dot(xb, wu_bf[:, D_FF + lo:D_FF + hi]) + bu_ref[0, :, D_FF + lo:D_FF + hi]
            glu = jnp.minimum(glu, SWIGLU_LIMIT)
            lin = jnp.clip(lin, -SWIGLU_LIMIT, SWIGLU_LIMIT)
            mid = (glu * jax.nn.sigmoid(SWIGLU_ALPHA * glu) * (lin + 1.0)).astype(BF16)
            part = _dot(mid, wd_bf[lo:hi, :])
            acc = part if acc is None else acc + part
        y_ref[...] = (acc + bd_ref[0]).astype(y_ref.dtype)

    @pl.when(jnp.logical_not(live))
    def _():
        y_ref[...] = jnp.zeros(y_ref.shape, y_ref.dtype)


def _experts(x_parts, blk_expert, n_used, w_up, b_up, w_down, b_down):
    n_part = x_parts[0].shape[0] // EXPERT_ROWS
    cap = n_part * EXPERT_ROWS * len(x_parts)
    ys = None
    for p, xs in enumerate(x_parts):
        first = p * n_part
        glob = lambda i, nu, first=first: jnp.minimum(i + first, nu[0] - 1)
        expert = lambda i, be, nu, glob=glob: (be[glob(i, nu)], 0, 0)
        in_specs = [pl.BlockSpec((EXPERT_ROWS, D_MODEL),
                                 lambda i, be, nu, glob=glob, first=first: (jnp.maximum(glob(i, nu) - first, 0), 0)),
                    pl.BlockSpec((1, D_MODEL, 2 * D_FF), expert), pl.BlockSpec((1, 1, 2 * D_FF), expert),
                    pl.BlockSpec((1, D_FF, D_MODEL), expert), pl.BlockSpec((1, 1, D_MODEL), expert)]
        operands = [blk_expert, n_used, xs, w_up, b_up, w_down, b_down]
        if ys is not None:
            in_specs.append(pl.BlockSpec(memory_space=pl.ANY))
            operands.append(ys)
        ys = pl.pallas_call(
            functools.partial(_expert_body, first, p > 0),
            grid_spec=pltpu.PrefetchScalarGridSpec(
                num_scalar_prefetch=2,
                grid=(n_part,),
                in_specs=in_specs,
                out_specs=pl.BlockSpec((EXPERT_ROWS, D_MODEL), lambda i, be, nu, first=first: (i + first, 0)),
                scratch_shapes=[pltpu.VMEM((D_MODEL, 2 * D_FF), BF16), pltpu.VMEM((D_FF, D_MODEL), BF16)]),
            out_shape=jax.ShapeDtypeStruct((cap, D_MODEL), BF16),
            input_output_aliases={len(operands) - 1: 0} if p > 0 else {},
            compiler_params=_params(("arbitrary",), EXPERT_VMEM_LIMIT),
            name="experts",
        )(*operands)
    return ys


def _combine_body(n_p, x1_ref, yg_ref, rt_ref, g_ref, yp_ref, ys_ref):
    i = pl.program_id(0)
    y = x1_ref[...]
    rt = rt_ref[...]
    for k in range(TOP_K):
        y = y + yg_ref[k] * rt[:, TOP_K + k:TOP_K + k + 1]
    ms = jnp.mean(y * y, axis=-1, keepdims=True)
    out = y * lax.rsqrt(ms + NORM_EPS) * g_ref[...]

    @pl.when(i < n_p)
    def _():
        yp_ref[...] = out

    @pl.when(i >= n_p)
    def _():
        ys_ref[...] = out


def _combine(x1, yg, rt, g, t_p, t_s):
    tm = min(512, t_s)
    assert t_p % tm == 0 and t_s % tm == 0
    n_p, n_s = t_p // tm, t_s // tm
    row = lambda i: (i, 0)
    return pl.pallas_call(
        functools.partial(_combine_body, n_p),
        grid=(n_p + n_s,),
        in_specs=[pl.BlockSpec((tm, D_MODEL), row), pl.BlockSpec((TOP_K, tm, D_MODEL), lambda i: (0, i, 0)),
                  pl.BlockSpec((tm, LANES), row), pl.BlockSpec((1, D_MODEL), lambda i: (0, 0))],
        out_specs=[pl.BlockSpec((tm, D_MODEL), lambda i: (jnp.minimum(i, n_p - 1), 0)),
                   pl.BlockSpec((tm, D_MODEL), lambda i: (jnp.maximum(i - n_p, 0), 0))],
        out_shape=[jax.ShapeDtypeStruct((t_p, D_MODEL), F32), jax.ShapeDtypeStruct((t_s, D_MODEL), F32)],
        compiler_params=_params(("arbitrary",)),
        name="combine",
    )(x1, yg, rt, g)


def _route(top_idx, rank):
    t = top_idx.shape[0]
    n = t * TOP_K
    bm = EXPERT_ROWS
    idx_bits = (n - 1).bit_length()
    assert N_EXPERTS << idx_bits < 2 ** 31
    e_flat = top_idx.reshape(-1)
    key = lax.sort(e_flat * (1 << idx_bits) + jnp.arange(n, dtype=jnp.int32))
    order = key & ((1 << idx_bits) - 1)
    experts = jnp.arange(N_EXPERTS, dtype=jnp.int32)
    onehot = e_flat[:, None] == experts[None, :]
    counts = jnp.sum(onehot.astype(jnp.int32), axis=0)
    padded = ((counts + bm - 1) // bm) * bm
    pad_end = jnp.cumsum(padded)
    pad_start = pad_end - padded
    grp_start = jnp.cumsum(counts) - counts
    n_blocks = -(-(-(-n // bm) + N_EXPERTS) // EXPERT_CALLS) * EXPERT_CALLS
    blk_start = jnp.arange(n_blocks, dtype=jnp.int32) * bm
    blk_expert = jnp.minimum(jnp.sum((pad_end[None, :] <= blk_start[:, None]).astype(jnp.int32), axis=1),
                             N_EXPERTS - 1)
    n_used = (pad_end[-1] // bm).astype(jnp.int32).reshape(1)
    shift = (pad_start - grp_start).astype(jnp.int32)
    row_shift = jnp.broadcast_to(shift[blk_expert][:, None], (n_blocks, bm)).reshape(-1)
    row_pos = jnp.clip(jnp.arange(n_blocks * bm, dtype=jnp.int32) - row_shift, 0, n - 1)
    row_tok = order[row_pos] // TOP_K
    dest = jnp.sum(jnp.where(onehot, pad_start[None, :], 0), axis=1) + rank.reshape(-1)
    return row_tok, blk_expert, n_used, dest.reshape(t, TOP_K).T.reshape(-1)


def kernel(x_prompt, x_sample, cache_k, cache_v, page_table, state_conv, state_C, state_n, state_m,
           norm_mix, w_in, conv_w, conv_b, gate_b, lstm_norm, w_out, norm_ffn, w_router, b_router,
           w_up, b_up, w_down, b_down, norm_final):
    assert w_in.shape[0] == 1, "single-layer stack"
    batch, seq, _ = x_prompt.shape
    db, ds, _ = x_sample.shape
    t_p, t_s = batch * seq, db * ds
    n_pool = cache_k.shape[1]
    assert seq % LSTM_CHUNK == 0 and page_table.shape[1] * PAGE_SIZE % MOBA_BLOCK == 0

    xp = x_prompt.reshape(t_p, D_MODEL)
    xs = x_sample.reshape(t_s, D_MODEL)
    w_main = w_in[0, :, :MAIN_COLS].astype(BF16)
    w_gate = jnp.pad(w_in[0, :, MAIN_COLS:], ((0, 0), (0, LANES - 2 * LSTM_HEADS))).astype(BF16)
    q_a, k_a, v_a, qk_src, v_l, o_l, gts, k_pages, v_pages = _front(xp, xs, norm_mix, w_main, w_gate)

    att_p = _moba(q_a, k_a, v_a, batch, seq)
    k_s, v_s = k_a[t_p:], v_a[t_p:]

    def key_minor(cache):
        cache = cache.reshape(n_pool, ATT_HEADS, PAGE_SIZE, ATT_HEAD_DIM).transpose(0, 1, 3, 2)
        return cache.reshape(n_pool, ATT_WIDTH, PAGE_SIZE)

    att_s = _samp_attn(q_a[t_p:], k_s, v_s, key_minor(cache_k), key_minor(cache_v), page_table, ds).astype(BF16)

    gate_b_row = jnp.pad(gate_b, ((0, 0), (0, LANES - 2 * LSTM_HEADS)))
    cw, cb = conv_w[0], conv_b
    hn_p, conv_p, c_p, n_p_, m_p = _mlstm(
        qk_src, v_l, o_l, gts,
        jnp.zeros((batch * SUBLANES, 2 * LSTM_WIDTH), F32),
        jnp.zeros((batch, LSTM_HEADS, LSTM_HEAD_DIM, LSTM_HEAD_DIM), F32),
        jnp.zeros((batch, SUBLANES, LANES), F32), jnp.zeros((batch, SUBLANES, LANES), F32),
        cw, cb, gate_b_row, lstm_norm, batch, seq // LSTM_CHUNK, LSTM_CHUNK)

    def pad_chunk(a):
        a = a[t_p:].reshape(db, ds, a.shape[-1])
        return jnp.pad(a, ((0, 0), (0, LSTM_CHUNK - ds), (0, 0))).reshape(db * LSTM_CHUNK, a.shape[-1])

    conv0_s = jnp.pad(state_conv[0], ((0, 0), (SUBLANES - (CONV_WIDTH - 1), 0), (0, 0)))
    n0_s = jnp.pad(state_n[0], ((0, 0), (0, SUBLANES - LSTM_HEADS), (0, 0)))
    m0_s = jnp.broadcast_to(jnp.pad(state_m[0], ((0, 0), (0, SUBLANES - LSTM_HEADS)))[:, :, None],
                            (db, SUBLANES, LANES))
    hn_s, conv_s, c_s, n_s, m_s = _mlstm(
        pad_chunk(qk_src), pad_chunk(v_l), pad_chunk(o_l), pad_chunk(gts),
        conv0_s.reshape(db * SUBLANES, 2 * LSTM_WIDTH), state_C[0], n0_s, m0_s,
        cw, cb, gate_b_row, lstm_norm, db, 1, ds)
    hn_s = hn_s.reshape(db, LSTM_CHUNK, LSTM_WIDTH)[:, :ds].reshape(t_s, LSTM_WIDTH)

    wo = w_out[0].astype(BF16)
    w_r = jnp.pad(w_router[0], ((0, 0), (0, LANES - N_EXPERTS))).astype(BF16)
    b_r = jnp.pad(b_router, ((0, 0), (0, LANES - N_EXPERTS)), constant_values=NEG)
    x1, h2, rt = _back(att_p, att_s, hn_p, hn_s, xp, xs, wo[:ATT_WIDTH], wo[ATT_WIDTH:], norm_ffn, w_r, b_r)

    row_tok, blk_expert, n_used, dest = _route(rt[:, :TOP_K].astype(jnp.int32),
                                               rt[:, 2 * TOP_K:3 * TOP_K].astype(jnp.int32))
    x_parts = [h2[piece] for piece in jnp.split(row_tok, EXPERT_CALLS)]
    ys = _experts(x_parts, blk_expert, n_used, w_up[0], b_up[0][:, None, :], w_down[0], b_down[0][:, None, :])
    yg = ys[dest].reshape(TOP_K, t_p + t_s, D_MODEL)
    y_p, y_s = _combine(x1, yg, rt, norm_final.reshape(1, D_MODEL), t_p, t_s)

    def to_pages(t):
        t = t.reshape(batch, seq // PAGE_SIZE, ATT_HEADS, ATT_HEAD_DIM, PAGE_SIZE)
        return t.transpose(0, 1, 2, 4, 3)[None]

    def to_slots(t):
        return t.reshape(db, ds, ATT_HEADS, ATT_HEAD_DIM).transpose(0, 2, 1, 3)[None]

    return (y_p.reshape(batch, seq, D_MODEL), y_s.reshape(db, ds, D_MODEL),
            to_pages(k_pages), to_pages(v_pages), to_slots(k_s), to_slots(v_s),
            conv_p[None], conv_s[None],
            c_p[None], n_p_[None, :, :LSTM_HEADS, :], m_p[None, :, :LSTM_HEADS, 0],
            c_s[None], n_s[None, :, :LSTM_HEADS, :], m_s[None, :, :LSTM_HEADS, 0])
```
